```python
import jax, jax.numpy as jnp
from jax import lax
import numpy as np

D_MODEL = 1024
BATCH = 8
SEQ = 8192
DEPTH = 2

HEAD_DIM = 64
N_HEADS_DIL = 8
N_HEADS_NA = 8
D_DIL = N_HEADS_DIL * HEAD_DIM
D_NA = N_HEADS_NA * HEAD_DIM
D_MIX = D_DIL + D_NA
ATTN_SCALE = HEAD_DIM ** -0.5
ROPE_THETA = 10000.0
DIL_PATTERNS = ((128, 1), (512, 4), (2048, 16))
DIL_BLOCK = 64
GRID_W = 64
NA_ROWS_MAX = 8
NA_COLS = 16
NA_QCOLS = 16
NA_KCOLS = NA_QCOLS + NA_COLS
N_EXPERTS = 16
EXPERT_FF = 2816
EC_CAPACITY = 2
EPS = 1e-6
NEG_INF = -1e30

kernel_name = "hybrid_dilated_neighbourhood_ec_moe_encoder"


def _rmsnorm(x, g):
    xf = x.astype(jnp.float32)
    y = xf * lax.rsqrt(jnp.mean(xf * xf, axis=-1, keepdims=True) + EPS)
    return (y * g.astype(jnp.float32)).astype(x.dtype)


def _rope_tables(T):
    pos = jnp.arange(T, dtype=jnp.float32)
    inv = ROPE_THETA ** (-jnp.arange(0, HEAD_DIM, 2, dtype=jnp.float32) / HEAD_DIM)
    ang = pos[:, None] * inv[None, :]
    return jnp.cos(ang), jnp.sin(ang)


def _apply_rope(x, cos, sin):
    xf = x.astype(jnp.float32)
    x1, x2 = jnp.split(xf, 2, axis=-1)
    return jnp.concatenate([x1 * cos - x2 * sin, x2 * cos + x1 * sin], axis=-1).astype(x.dtype)


def _heads(a, n):
    B, T, _ = a.shape
    return a.reshape(B, T, n, HEAD_DIM).transpose(0, 2, 1, 3)


def _unheads(a):
    B, H, T, hd = a.shape
    return a.transpose(0, 2, 1, 3).reshape(B, T, H * hd)


def _dilated_branch(q, k, v, window, dilation):
    B, H, T, hd = q.shape
    half = window // (2 * dilation)
    L = T // dilation
    blk = DIL_BLOCK
    nb = -(-L // blk)
    nbr = -(-half // blk)
    span = (2 * nbr + 1) * blk
    Lp = nb * blk

    def strided(a):
        return a.reshape(B, H, L, dilation, hd).transpose(0, 1, 3, 2, 4)

    qs = jnp.pad(strided(q), ((0, 0), (0, 0), (0, 0), (0, Lp - L), (0, 0)))
    qs = qs.reshape(B, H, dilation, nb, blk, hd)
    pad_k = ((0, 0), (0, 0), (0, 0), (nbr * blk, Lp - L + nbr * blk), (0, 0))

    def windows(a):
        ab = jnp.pad(strided(a), pad_k).reshape(B, H, dilation, nb + 2 * nbr, blk, hd)
        return jnp.concatenate([ab[:, :, :, sh:sh + nb] for sh in range(2 * nbr + 1)], axis=4)

    ks, vs = windows(k), windows(v)
    q_idx = jnp.arange(nb)[:, None] * blk + jnp.arange(blk)[None, :]
    k_idx = jnp.arange(nb)[:, None] * blk + jnp.arange(span)[None, :] - nbr * blk
    rel = k_idx[:, None, :] - q_idx[:, :, None]
    valid = (jnp.abs(rel) <= half) & (k_idx[:, None, :] >= 0) & (k_idx[:, None, :] < L)

    s = jnp.einsum('bhrnqd,bhrnkd->bhrnqk', qs, ks, preferred_element_type=jnp.float32) * ATTN_SCALE
    s = jnp.where(valid, s, NEG_INF)
    m = jnp.max(s, axis=-1, keepdims=True)
    p = jnp.exp(s - m)
    den = jnp.sum(p, axis=-1, keepdims=True)
    o = jnp.einsum('bhrnqk,bhrnkd->bhrnqd', p.astype(v.dtype), vs,
                   preferred_element_type=jnp.float32) / den
    lse = (m + jnp.log(den))[..., 0]

    def unstrided(a):
        a = a.reshape(B, H, dilation, Lp, *a.shape[5:])[:, :, :, :L]
        return jnp.moveaxis(a, 2, 3).reshape(B, H, T, *a.shape[4:])

    return unstrided(o), unstrided(lse)


def _dilated_attention(q, k, v):
    outs, lses = zip(*[_dilated_branch(q, k, v, w, d) for (w, d) in DIL_PATTERNS])
    wts = jax.nn.softmax(jnp.stack(lses), axis=0)
    return jnp.sum(wts[..., None] * jnp.stack(outs), axis=0)


def _neighbourhood_attention(q, k, v, rpb):
    B, H, T, hd = q.shape
    rows = T // GRID_W
    kh = min(NA_ROWS_MAX, rows)
    kw = NA_COLS
    ncb = GRID_W // NA_QCOLS
    kg = k.reshape(B, H, rows, GRID_W, hd)
    vg = v.reshape(B, H, rows, GRID_W, hd)
    q_rows = jnp.moveaxis(q.reshape(B, H, rows, GRID_W, hd), 2, 0)

    r_idx = jnp.arange(rows)
    r_start = jnp.clip(r_idx - kh // 2, 0, rows - kh)
    c_idx = jnp.arange(GRID_W).reshape(ncb, NA_QCOLS)
    c_start = jnp.clip(c_idx - kw // 2, 0, GRID_W - kw)
    kc_start = jnp.clip(jnp.arange(ncb) * NA_QCOLS - kw // 2, 0, GRID_W - NA_KCOLS)
    kc = kc_start[:, None] + jnp.arange(NA_KCOLS)[None, :]
    col_valid = (kc[:, None, :] >= c_start[..., None]) & (kc[:, None, :] < c_start[..., None] + kw)
    valid = jnp.broadcast_to(col_valid[:, :, None, :], (ncb, NA_QCOLS, kh, NA_KCOLS))
    valid = valid.reshape(ncb, NA_QCOLS, kh * NA_KCOLS)
    dcol = jnp.clip(kc[:, None, :] - c_idx[..., None] + (kw - 1), 0, 2 * kw - 2)
    rpb_f = rpb.astype(jnp.float32)

    def one_row(args):
        q_r, r, rs = args
        k_r = lax.dynamic_slice_in_dim(kg, rs, kh, axis=2)
        v_r = lax.dynamic_slice_in_dim(vg, rs, kh, axis=2)

        def blocks(a):
            a = a[:, :, :, kc]
            return a.transpose(0, 1, 3, 2, 4, 5).reshape(B, H, ncb, kh * NA_KCOLS, hd)

        kb, vb = blocks(k_r), blocks(v_r)
        drow = rs + jnp.arange(kh) - r + (NA_ROWS_MAX - 1)
        bias = rpb_f[:, drow[None, None, :, None], dcol[:, :, None, :]]
        bias = bias.reshape(H, ncb, NA_QCOLS, kh * NA_KCOLS)
        qb = q_r.reshape(B, H, ncb, NA_QCOLS, hd)
        s = jnp.einsum('bhjqd,bhjkd->bhjqk', qb, kb, preferred_element_type=jnp.float32) * ATTN_SCALE
        s = jnp.where(valid, s + bias, NEG_INF)
        p = jax.nn.softmax(s, axis=-1)
        o = jnp.einsum('bhjqk,bhjkd->bhjqd', p.astype(v.dtype), vb, preferred_element_type=jnp.float32)
        return o.reshape(B, H, GRID_W, hd)

    o = lax.map(one_row, (q_rows, r_idx, r_start))
    return jnp.moveaxis(o, 0, 2).reshape(B, H, T, hd)


def _expert_choice_ffn(h, w_router, w_gate, w_up, w_down):
    B, T, D = h.shape
    cap = EC_CAPACITY * T // N_EXPERTS
    logits = jnp.einsum('btd,de->bte', h, w_router, preferred_element_type=jnp.float32)
    aff = jax.nn.softmax(logits, axis=-1)
    gate, idx = lax.top_k(aff.transpose(0, 2, 1), cap)
    bidx = jnp.arange(B)[:, None, None]
    xg = h[bidx, idx]

    def expert(args):
        xe, wg, wu, wd = args
        return (jax.nn.silu(xe @ wg) * (xe @ wu)) @ wd

    ye = lax.map(expert, (xg.transpose(1, 0, 2, 3), w_gate, w_up, w_down))
    ye = ye.transpose(1, 0, 2, 3) * gate[..., None].astype(h.dtype)
    return jnp.zeros_like(h).at[bidx, idx].add(ye)


def setup_inputs(seed: int = 0) -> dict:
    key = jax.random.key(seed)
    ks = jax.random.split(key, 13)
    f32 = jnp.float32
    nrm = lambda k, shape: jax.random.normal(k, shape, dtype=f32)
    return {
        "x": nrm(ks[0], (BATCH, SEQ, D_MODEL)),
        "attn_norm": 1.0 + 0.05 * nrm(ks[1], (DEPTH, D_MODEL)),
        "w_in": nrm(ks[2], (DEPTH, D_MODEL, 3 * D_MIX)) * D_MODEL ** -0.5,
        "dil_out_norm": 1.0 + 0.05 * nrm(ks[3], (DEPTH, D_DIL)),
        "na_out_norm": 1.0 + 0.05 * nrm(ks[4], (DEPTH, D_NA)),
        "na_rpb": 0.1 * nrm(ks[5], (DEPTH, N_HEADS_NA, 2 * NA_ROWS_MAX - 1, 2 * NA_COLS - 1)),
        "w_out": nrm(ks[6], (DEPTH, D_MIX, D_MODEL)) * D_MIX ** -0.5,
        "ffn_norm": 1.0 + 0.05 * nrm(ks[7], (DEPTH, D_MODEL)),
        "w_router": nrm(ks[8], (DEPTH, D_MODEL, N_EXPERTS)) * D_MODEL ** -0.5,
        "w_gate": nrm(ks[9], (DEPTH, N_EXPERTS, D_MODEL, EXPERT_FF)) * D_MODEL ** -0.5,
        "w_up": nrm(ks[10], (DEPTH, N_EXPERTS, D_MODEL, EXPERT_FF)) * D_MODEL ** -0.5,
        "w_down": nrm(ks[11], (DEPTH, N_EXPERTS, EXPERT_FF, D_MODEL)) * EXPERT_FF ** -0.5,
        "final_norm": 1.0 + 0.05 * nrm(ks[12], (D_MODEL,)),
    }


def reference(x, attn_norm, w_in, dil_out_norm, na_out_norm, na_rpb, w_out,
              ffn_norm, w_router, w_gate, w_up, w_down, final_norm):
    B, T, _ = x.shape
    cos, sin = _rope_tables(T)
    split_pts = (D_DIL, 2 * D_DIL, 3 * D_DIL, 3 * D_DIL + D_NA, 3 * D_DIL + 2 * D_NA)
    for l in range(DEPTH):
        h = _rmsnorm(x, attn_norm[l])
        proj = jnp.einsum('btd,df->btf', h, w_in[l])
        qa, ka, va, qb, kb, vb = jnp.split(proj, split_pts, axis=-1)
        qa = _apply_rope(_heads(qa, N_HEADS_DIL), cos, sin)
        ka = _apply_rope(_heads(ka, N_HEADS_DIL), cos, sin)
        oa = _dilated_attention(qa, ka, _heads(va, N_HEADS_DIL))
        ob = _neighbourhood_attention(_heads(qb, N_HEADS_NA), _heads(kb, N_HEADS_NA),
                                      _heads(vb, N_HEADS_NA), na_rpb[l])
        mixed = jnp.concatenate([_rmsnorm(_unheads(oa).astype(x.dtype), dil_out_norm[l]),
                                 _rmsnorm(_unheads(ob).astype(x.dtype), na_out_norm[l])], axis=-1)
        x = x + jnp.einsum('btf,fd->btd', mixed, w_out[l])
        x = x + _expert_choice_ffn(_rmsnorm(x, ffn_norm[l]), w_router[l], w_gate[l], w_up[l], w_down[l])
    return _rmsnorm(x, final_norm)
```

```python
import functools

import jax
import jax.numpy as jnp
from jax import lax
from jax.experimental import pallas as pl
from jax.experimental.pallas import tpu as pltpu

F32 = jnp.float32
BF16 = jnp.bfloat16
I32 = jnp.int32

HEAD_DIM = 64
LANES = 128
PAIRS = 4
GROUP = PAIRS * LANES
ATTN_SCALE = HEAD_DIM ** -0.5
ROPE_THETA = 10000.0
DIL_PATTERNS = ((128, 1), (512, 4), (2048, 16))
GRID_W = 64
NA_ROWS = 8
NA_COLS = 16
N_EXPERTS = 16
EC_CAPACITY = 2
EPS = 1e-6
NEG_INF = -1e30

VMEM_LIMIT = 56 * 1024 * 1024

_NT = (((1,), (1,)), ((), ()))


def _cparams(sem, vmem=VMEM_LIMIT):
    return pltpu.CompilerParams(dimension_semantics=sem, vmem_limit_bytes=vmem)


def _head0_mask():
    return lax.broadcasted_iota(I32, (1, LANES), 1) < HEAD_DIM


def _qkv_kernel(x_ref, g_ref, w_ref, cos_ref, sin_ref,
                qa_ref, ka_ref, va_ref, qb_ref, kb_ref, vb_ref):
    x = x_ref[...]
    ms = jnp.mean(x * x, axis=-1, keepdims=True)
    h = (x * lax.rsqrt(ms + EPS) * g_ref[...]).astype(BF16)
    cos = cos_ref[...]
    sin = sin_ref[...]
    lane = lax.broadcasted_iota(I32, (1, LANES), 1)
    first_half = (lane % HEAD_DIM) < HEAD_DIM // 2
    outs = (qa_ref, ka_ref, va_ref, qb_ref, kb_ref, vb_ref)
    for gi, o_ref in enumerate(outs):
        y = jnp.dot(h, w_ref[:, gi * GROUP:(gi + 1) * GROUP], preferred_element_type=F32)
        for p in range(PAIRS):
            s = y[:, p * LANES:(p + 1) * LANES]
            if gi < 2:
                partner = jnp.where(first_half, pltpu.roll(s, LANES - HEAD_DIM // 2, 1),
                                    pltpu.roll(s, HEAD_DIM // 2, 1))
                s = s * cos + partner * sin
            if gi in (0, 3):
                s = s * ATTN_SCALE
            o_ref[p] = s.astype(o_ref.dtype)


def _qkv(x, g, w_bf16, cos, sin, tm=512):
    B, T, D = x.shape
    tm = min(tm, T)
    slab = lambda dt: jax.ShapeDtypeStruct((B, PAIRS, T, LANES), dt)
    out_spec = pl.BlockSpec((None, PAIRS, tm, LANES), lambda b, i: (b, 0, i, 0))
    return pl.pallas_call(
        _qkv_kernel,
        grid=(B, T // tm),
        in_specs=[
            pl.BlockSpec((None, tm, D), lambda b, i: (b, i, 0)),
            pl.BlockSpec((1, D), lambda b, i: (0, 0)),
            pl.BlockSpec((D, 6 * GROUP), lambda b, i: (0, 0)),
            pl.BlockSpec((tm, LANES), lambda b, i: (i, 0)),
            pl.BlockSpec((tm, LANES), lambda b, i: (i, 0)),
        ],
        out_specs=[out_spec] * 6,
        out_shape=[slab(F32), slab(F32), slab(F32), slab(BF16), slab(BF16), slab(BF16)],
        compiler_params=_cparams(("parallel", "parallel")),
        name="qkv",
    )(x, g.reshape(1, D), w_bf16, cos, sin)


def _dil_kernel(q_ref, k_ref, v_ref, o_ref, m_ref, l_ref, *, T, Lq):
    head0 = _head0_mask()
    n_br = len(DIL_PATTERNS)
    for bi, (window, d) in enumerate(DIL_PATTERNS):
        half = window // (2 * d)
        L = T // d
        Wn = Lq + 2 * half
        nb = L // Lq
        rel0 = (lax.broadcasted_iota(I32, (Lq, Wn), 1) - lax.broadcasted_iota(I32, (Lq, Wn), 0))

        def block(idx, carry, bi=bi, d=d, half=half, L=L, Wn=Wn, nb=nb, rel0=rel0):
            r = idx // nb
            m0 = (idx % nb) * Lq
            ws = jnp.clip(m0 - half, 0, L - Wn)

            def rows(start, size):
                if d == 1:
                    return pl.ds(start, size)
                return pl.ds(r + d * start, size, stride=d)

            q = q_ref[rows(m0, Lq), :]
            k = k_ref[rows(ws, Wn), :].astype(BF16)
            v = v_ref[rows(ws, Wn), :].astype(BF16)
            valid = jnp.abs(rel0 + (ws - m0)) <= half
            parts = []
            for hsel in (head0, jnp.logical_not(head0)):
                qh = jnp.where(hsel, q, 0.0).astype(BF16)
                s = lax.dot_general(qh, k, _NT, preferred_element_type=F32)
                s = jnp.where(valid, s, NEG_INF)
                mh = jnp.max(s, axis=-1, keepdims=True)
                p = jnp.exp(s - mh)
                lh = jnp.sum(p, axis=-1, keepdims=True)
                pv = jnp.dot(p.astype(BF16), v, preferred_element_type=F32)
                parts.append((mh, lh, pv))
            mb = jnp.where(head0, parts[0][0], parts[1][0])
            lb = jnp.where(head0, parts[0][1], parts[1][1])
            ab = jnp.where(head0, parts[0][2], parts[1][2])
            qrows = rows(m0, Lq)
            if bi == 0:
                m_ref[qrows, :] = mb
                l_ref[qrows, :] = lb
                o_ref[qrows, :] = ab
            else:
                mo = m_ref[qrows, :]
                mn = jnp.maximum(mo, mb)
                al = jnp.exp(mo - mn)
                be = jnp.exp(mb - mn)
                ln = al * l_ref[qrows, :] + be * lb
                an = al * o_ref[qrows, :] + be * ab
                if bi == n_br - 1:
                    o_ref[qrows, :] = an / ln
                else:
                    m_ref[qrows, :] = mn
                    l_ref[qrows, :] = ln
                    o_ref[qrows, :] = an
            return carry

        lax.fori_loop(0, d * nb, block, 0)


def _dilated(qa, ka, va, Lq=128):
    B, P, T, _ = qa.shape
    for window, d in DIL_PATTERNS:
        L = T // d
        assert T % d == 0 and L % Lq == 0 and Lq + window // d <= L, (T, window, d)
    spec = pl.BlockSpec((None, None, T, LANES), lambda b, p: (b, p, 0, 0))
    return pl.pallas_call(
        functools.partial(_dil_kernel, T=T, Lq=Lq),
        grid=(B, P),
        in_specs=[spec, spec, spec],
        out_specs=spec,
        out_shape=jax.ShapeDtypeStruct((B, P, T, LANES), F32),
        scratch_shapes=[pltpu.VMEM((T, LANES), F32), pltpu.VMEM((T, LANES), F32)],
        compiler_params=_cparams(("parallel", "parallel")),
        name="dilated",
    )(qa, ka, va)


def _na_bias_tables(rpb):
    H = rpb.shape[0]
    qc = jnp.arange(GRID_W)
    kc = jnp.arange(GRID_W)
    kr = jnp.arange(NA_ROWS)
    shift = jnp.arange(NA_ROWS)
    cs = jnp.clip(qc - NA_COLS // 2, 0, GRID_W - NA_COLS)
    valid = (kc[None, :] >= cs[:, None]) & (kc[None, :] < cs[:, None] + NA_COLS)
    dcol = jnp.clip(kc[None, :] - qc[:, None] + (NA_COLS - 1), 0, 2 * NA_COLS - 2)
    drow = kr[None, :] - shift[:, None] + (NA_ROWS - 1)
    tbl = rpb.astype(F32)[:, drow[:, None, :, None], dcol[None, :, None, :]]
    tbl = jnp.where(valid[None, None, :, None, :], tbl, NEG_INF)
    return tbl.reshape(H // 2, 2, NA_ROWS, GRID_W, NA_ROWS * GRID_W)


def _na_kernel(q_ref, k_ref, v_ref, tbl_ref, o_ref, *, rows):
    head0 = _head0_mask()

    def body(r, carry):
        rs = jnp.clip(r - NA_ROWS // 2, 0, rows - NA_ROWS)
        shift = r - rs
        q = q_ref[pl.ds(pl.multiple_of(r * GRID_W, GRID_W), GRID_W), :]
        kwin = pl.ds(pl.multiple_of(rs * GRID_W, GRID_W), NA_ROWS * GRID_W)
        k = k_ref[kwin, :]
        v = v_ref[kwin, :]
        res = []
        for hi, hsel in enumerate((head0, jnp.logical_not(head0))):
            qh = jnp.where(hsel, q, jnp.zeros_like(q))
            s = lax.dot_general(qh, k, _NT, preferred_element_type=F32)
            s = s + tbl_ref[hi, pl.ds(shift, 1)][0]
            m = jnp.max(s, axis=-1, keepdims=True)
            p = jnp.exp(s - m)
            l = jnp.sum(p, axis=-1, keepdims=True)
            res.append(jnp.dot(p.astype(BF16), v, preferred_element_type=F32) / l)
        o_ref[pl.ds(pl.multiple_of(r * GRID_W, GRID_W), GRID_W), :] = jnp.where(head0, res[0], res[1])
        return carry

    lax.fori_loop(0, rows, body, 0)


def _neighbourhood(qb, kb, vb, tbl):
    B, P, T, _ = qb.shape
    rows = T // GRID_W
    assert T % GRID_W == 0 and rows >= NA_ROWS
    spec = pl.BlockSpec((None, None, T, LANES), lambda b, p: (b, p, 0, 0))
    return pl.pallas_call(
        functools.partial(_na_kernel, rows=rows),
        grid=(B, P),
        in_specs=[spec, spec, spec,
                  pl.BlockSpec((None, 2, NA_ROWS, GRID_W, NA_ROWS * GRID_W), lambda b, p: (p, 0, 0, 0, 0))],
        out_specs=spec,
        out_shape=jax.ShapeDtypeStruct((B, P, T, LANES), F32),
        compiler_params=_cparams(("parallel", "parallel")),
        name="nbr",
    )(qb, kb, vb, tbl)


def _rms(a, g):
    ms = jnp.mean(a * a, axis=-1, keepdims=True)
    return a * lax.rsqrt(ms + EPS) * g


def _post_kernel(od_ref, on_ref, x_ref, wo_ref, gd_ref, gn_ref, gf_ref, wrh_ref, wrl_ref,
                 x1_ref, h_ref, aff_ref):
    dil = jnp.concatenate([od_ref[p] for p in range(PAIRS)], axis=-1)
    nbr = jnp.concatenate([on_ref[p] for p in range(PAIRS)], axis=-1)
    mixed = jnp.concatenate([_rms(dil, gd_ref[...]), _rms(nbr, gn_ref[...])], axis=-1).astype(BF16)
    x1 = x_ref[...] + jnp.dot(mixed, wo_ref[...], preferred_element_type=F32)
    x1_ref[...] = x1
    h = _rms(x1, gf_ref[...])
    hh = h.astype(BF16)
    hl = (h - hh.astype(F32)).astype(BF16)
    h_ref[...] = hh
    wrh = wrh_ref[...]
    lt = (lax.dot_general(wrh, hh, _NT, preferred_element_type=F32)
          + lax.dot_general(wrh, hl, _NT, preferred_element_type=F32)
          + lax.dot_general(wrl_ref[...], hh, _NT, preferred_element_type=F32))
    m = jnp.max(lt, axis=0, keepdims=True)
    e = jnp.exp(lt - m)
    aff_ref[...] = e / jnp.sum(e, axis=0, keepdims=True)


def _post(od, on, x, wo_bf16, gd, gn, gf, wr_hi, wr_lo, tm=512):
    B, T, D = x.shape
    tm = min(tm, T)
    E = wr_hi.shape[0]
    slab = pl.BlockSpec((None, PAIRS, tm, LANES), lambda b, i: (b, 0, i, 0))
    tok = pl.BlockSpec((None, tm, D), lambda b, i: (b, i, 0))
    full = lambda shape: pl.BlockSpec(shape, lambda b, i: (0,) * len(shape))
    return pl.pallas_call(
        _post_kernel,
        grid=(B, T // tm),
        in_specs=[slab, slab, tok, full((2 * GROUP, D)), full((1, GROUP)), full((1, GROUP)),
                  full((1, D)), full((E, D)), full((E, D))],
        out_specs=[tok, tok, pl.BlockSpec((None, E, tm), lambda b, i: (b, 0, i))],
        out_shape=[jax.ShapeDtypeStruct((B, T, D), F32), jax.ShapeDtypeStruct((B, T, D), BF16),
                   jax.ShapeDtypeStruct((B, E, T), F32)],
        compiler_params=_cparams(("parallel", "parallel")),
        name="post",
    )(od, on, x, wo_bf16, gd.reshape(1, GROUP), gn.reshape(1, GROUP), gf.reshape(1, D), wr_hi, wr_lo)


def _select_kernel(aff_ref, pos_ref, pref_ref, *, T, C):
    E = aff_ref.shape[0]
    bits = pltpu.bitcast(aff_ref[...], I32)
    t = jnp.zeros((E, 1), I32)
    for bit in range(30, -1, -1):
        cand = t | (1 << bit)
        cnt = jnp.sum(jnp.where(bits >= cand, 1.0, 0.0), axis=1, keepdims=True)
        t = jnp.where(cnt >= C, cand, t)
    gt = bits > t
    eq = bits == t
    need = C - jnp.sum(jnp.where(gt, 1.0, 0.0), axis=1, keepdims=True)
    tri = jnp.where(lax.broadcasted_iota(I32, (LANES, LANES), 0) < lax.broadcasted_iota(I32, (LANES, LANES), 1),
                    1.0, 0.0).astype(BF16)
    eq_carry = jnp.zeros((E, 1), F32)
    sel_carry = jnp.zeros((E, 1), F32)
    for c in range(T // LANES):
        sl = slice(c * LANES, (c + 1) * LANES)
        eqc = jnp.where(eq[:, sl], 1.0, 0.0)
        eq_rank = jnp.dot(eqc.astype(BF16), tri, preferred_element_type=F32) + eq_carry
        eq_carry = eq_carry + jnp.sum(eqc, axis=1, keepdims=True)
        sel = jnp.logical_or(gt[:, sl], jnp.logical_and(eq[:, sl], eq_rank < need))
        selc = jnp.where(sel, 1.0, 0.0)
        pref = (jnp.dot(selc.astype(BF16), tri, preferred_element_type=F32) + sel_carry).astype(I32)
        sel_carry = sel_carry + jnp.sum(selc, axis=1, keepdims=True)
        pref_ref[:, sl] = pref
        pos_ref[:, sl] = jnp.where(sel, pref, -1)


def _select(aff_t, C):
    B, E, T = aff_t.shape
    spec = pl.BlockSpec((None, E, T), lambda b: (b, 0, 0))
    return pl.pallas_call(
        functools.partial(_select_kernel, T=T, C=C),
        grid=(B,),
        in_specs=[spec],
        out_specs=[spec, spec],
        out_shape=[jax.ShapeDtypeStruct((B, E, T), I32), jax.ShapeDtypeStruct((B, E, T), I32)],
        compiler_params=_cparams(("parallel",)),
        name="select",
    )(aff_t)


def _dispatch_kernel(cc_ref, pos_ref, aff_ref, h_ref, xg_ref, gate_ref, acc_ref, gacc_ref, *, C, Cs, Tc, NC):
    E = pl.num_programs(1)
    row = (pl.program_id(0) * E + pl.program_id(1)) * (NC + 1)
    for j in range(C // Cs):
        base = j * Cs

        def count(c, carry, base=base):
            lo, hi = carry
            lo = lo + (cc_ref[row + c + 1] <= base).astype(I32)
            hi = hi + (cc_ref[row + c] < base + Cs).astype(I32)
            return lo, hi

        c_lo, c_hi = lax.fori_loop(0, NC, count, (jnp.int32(0), jnp.int32(0)))
        acc_ref[...] = jnp.zeros_like(acc_ref)
        gacc_ref[...] = jnp.zeros_like(gacc_ref)
        slot = base + lax.broadcasted_iota(I32, (Cs, Tc), 0)

        def chunk(c, carry, slot=slot):
            hit = pos_ref[pl.ds(c, 1), :] == slot
            hc = h_ref[pl.ds(pl.multiple_of(c * Tc, Tc), Tc), :]
            acc_ref[...] += jnp.dot(jnp.where(hit, 1.0, 0.0).astype(BF16), hc, preferred_element_type=F32)
            gacc_ref[...] += jnp.sum(jnp.where(hit, aff_ref[pl.ds(c, 1), :], 0.0), axis=1, keepdims=True)
            return carry

        lax.fori_loop(c_lo, c_hi, chunk, 0)
        xg_ref[base:base + Cs, :] = acc_ref[...].astype(BF16)
        gate_ref[base:base + Cs, :] = jnp.broadcast_to(gacc_ref[...], (Cs, LANES))


def _dispatch(cc, pos, aff_t, h, C, Cs=128, Tc=256):
    B, E, T = pos.shape
    D = h.shape[-1]
    NC = T // Tc
    assert T % Tc == 0 and C % Cs == 0
    row_spec = pl.BlockSpec((None, None, NC, Tc), lambda b, e, cc: (b, e, 0, 0))
    return pl.pallas_call(
        functools.partial(_dispatch_kernel, C=C, Cs=Cs, Tc=Tc, NC=NC),
        grid_spec=pltpu.PrefetchScalarGridSpec(
            num_scalar_prefetch=1,
            grid=(B, E),
            in_specs=[row_spec, row_spec,
                      pl.BlockSpec((None, T, D), lambda b, e, cc: (b, 0, 0), pipeline_mode=pl.Buffered(1))],
            out_specs=[pl.BlockSpec((None, None, C, D), lambda b, e, cc: (e, b, 0, 0)),
                       pl.BlockSpec((None, None, C, LANES), lambda b, e, cc: (e, b, 0, 0))],
            scratch_shapes=[pltpu.VMEM((Cs, D), F32), pltpu.VMEM((Cs, 1), F32)],
        ),
        out_shape=[jax.ShapeDtypeStruct((E, B, C, D), BF16), jax.ShapeDtypeStruct((E, B, C, LANES), F32)],
        compiler_params=_cparams(("arbitrary", "arbitrary")),
        name="dispatch",
    )(cc, pos.reshape(B, E, NC, Tc), aff_t.reshape(B, E, NC, Tc), h)


def _ffn_kernel(x_ref, wg_ref, wu_ref, wd_ref, gate_ref, y_ref, acc_ref, *, FF, fc):
    x = x_ref[...]
    for kc in range(FF // fc):
        sl = slice(kc * fc, (kc + 1) * fc)
        g = jnp.dot(x, wg_ref[:, sl], preferred_element_type=F32)
        u = jnp.dot(x, wu_ref[:, sl], preferred_element_type=F32)
        a = (g * jax.nn.sigmoid(g) * u).astype(BF16)
        y = jnp.dot(a, wd_ref[sl, :], preferred_element_type=F32)
        if kc == 0:
            acc_ref[...] = y
        else:
            acc_ref[...] += y
    gate = gate_ref[...]
    for s in range(y_ref.shape[-1] // LANES):
        sl = slice(s * LANES, (s + 1) * LANES)
        y_ref[:, sl] = (acc_ref[:, sl] * gate).astype(y_ref.dtype)


def _ffn(xg, wg, wu, wd, gates, tm=512, fc=256):
    E, R, D = xg.shape
    FF = wg.shape[-1]
    assert R % tm == 0 and FF % fc == 0
    return pl.pallas_call(
        functools.partial(_ffn_kernel, FF=FF, fc=fc),
        grid=(E, R // tm),
        in_specs=[pl.BlockSpec((None, tm, D), lambda e, i: (e, i, 0)),
                  pl.BlockSpec((None, D, FF), lambda e, i: (e, 0, 0)),
                  pl.BlockSpec((None, D, FF), lambda e, i: (e, 0, 0)),
                  pl.BlockSpec((None, FF, D), lambda e, i: (e, 0, 0)),
                  pl.BlockSpec((None, tm, LANES), lambda e, i: (e, i, 0))],
        out_specs=pl.BlockSpec((None, tm, D), lambda e, i: (e, i, 0)),
        out_shape=jax.ShapeDtypeStruct((E, R, D), BF16),
        scratch_shapes=[pltpu.VMEM((tm, D), F32)],
        compiler_params=_cparams(("parallel", "parallel")),
        name="ffn",
    )(xg, wg, wu, wd, gates)


def _combine_kernel(ss_ref, pos_ref, x1_ref, ye_ref, gfin_ref, out_ref, acc_ref, *, C, W, NT, final):
    E = ye_ref.shape[0]
    tm = x1_ref.shape[0]
    b = pl.program_id(0)
    i = pl.program_id(1)
    acc_ref[...] = x1_ref[...]
    lane = lax.broadcasted_iota(I32, (tm, W), 1)
    for e in range(E):
        row = (b * E + e) * (NT + 1) + i
        s0 = ss_ref[row]
        s1 = ss_ref[row + 1]
        a0 = (s0 // 16) * 16
        n_chunk = jnp.where(s1 > s0, (s1 - a0 + W - 1) // W, 0)
        pcol = pos_ref[:, e:e + 1]

        def chunk(k, carry, e=e, a0=a0, pcol=pcol):
            lo = a0 + k * W
            a = pl.multiple_of(jnp.minimum(lo, C - W), 16)
            slot = a + lane
            hit = jnp.logical_and(pcol == slot, slot >= lo)
            y = ye_ref[e, pl.ds(a, W), :]
            acc_ref[...] += jnp.dot(jnp.where(hit, 1.0, 0.0).astype(BF16), y, preferred_element_type=F32)
            return carry

        lax.fori_loop(0, n_chunk, chunk, 0)
    out = acc_ref[...]
    if final:
        out = _rms(out, gfin_ref[...])
    out_ref[...] = out


def _combine(ss, pos_t, x1, ye, gfin, C, final, tm=256, W=64):
    B, T, D = x1.shape
    E = ye.shape[0]
    tm = min(tm, T)
    NT = T // tm
    assert C % 16 == 0 and W % 16 == 0 and W <= C
    return pl.pallas_call(
        functools.partial(_combine_kernel, C=C, W=W, NT=NT, final=final),
        grid_spec=pltpu.PrefetchScalarGridSpec(
            num_scalar_prefetch=1,
            grid=(B, NT),
            in_specs=[pl.BlockSpec((None, tm, E), lambda b, i, ss: (b, i, 0)),
                      pl.BlockSpec((None, tm, D), lambda b, i, ss: (b, i, 0)),
                      pl.BlockSpec((E, None, C, D), lambda b, i, ss: (0, b, 0, 0), pipeline_mode=pl.Buffered(1)),
                      pl.BlockSpec((1, D), lambda b, i, ss: (0, 0))],
            out_specs=pl.BlockSpec((None, tm, D), lambda b, i, ss: (b, i, 0)),
            scratch_shapes=[pltpu.VMEM((tm, D), F32)],
        ),
        out_shape=jax.ShapeDtypeStruct((B, T, D), F32),
        compiler_params=_cparams(("arbitrary", "arbitrary")),
        name="combine",
    )(ss, pos_t, x1, ye, gfin.reshape(1, D))


def _rope_tables(T):
    pos = jnp.arange(T, dtype=F32)
    inv = ROPE_THETA ** (-jnp.arange(0, HEAD_DIM, 2, dtype=F32) / HEAD_DIM)
    ang = pos[:, None] * inv[None, :]
    cos = jnp.tile(jnp.cos(ang), (1, LANES // (HEAD_DIM // 2)))
    sin = jnp.sin(ang)
    sin = jnp.tile(jnp.concatenate([-sin, sin], axis=-1), (1, LANES // HEAD_DIM))
    return cos, sin


def _chunk_starts(pref, step, total):
    B, E, _ = pref.shape
    tail = jnp.full((B, E, 1), total, I32)
    return jnp.concatenate([pref[:, :, ::step], tail], axis=-1).reshape(-1)


def kernel(x, attn_norm, w_in, dil_out_norm, na_out_norm, na_rpb, w_out, ffn_norm, w_router,
           w_gate, w_up, w_down, final_norm):
    B, T, D = x.shape
    depth = w_in.shape[0]
    C = EC_CAPACITY * T // N_EXPERTS
    cos, sin = _rope_tables(T)
    tc_dispatch, tm_combine = 256, min(256, T)
    for l in range(depth):
        qa, ka, va, qb, kb, vb = _qkv(x, attn_norm[l], w_in[l].astype(BF16), cos, sin)
        od = _dilated(qa, ka, va)
        on = _neighbourhood(qb, kb, vb, _na_bias_tables(na_rpb[l]))
        wr_t = w_router[l].T
        wr_hi = wr_t.astype(BF16)
        wr_lo = (wr_t - wr_hi.astype(F32)).astype(BF16)
        x1, h, aff_t = _post(od, on, x, w_out[l].astype(BF16), dil_out_norm[l], na_out_norm[l],
                             ffn_norm[l], wr_hi, wr_lo)
        pos, pref = _select(aff_t, C)
        xg, gates = _dispatch(_chunk_starts(pref, tc_dispatch, C), pos, aff_t, h, C, Tc=tc_dispatch)
        ye = _ffn(xg.reshape(N_EXPERTS, B * C, D), w_gate[l].astype(BF16), w_up[l].astype(BF16),
                  w_down[l].astype(BF16), gates.reshape(N_EXPERTS, B * C, LANES))
        x = _combine(_chunk_starts(pref, tm_combine, C), pos.transpose(0, 2, 1), x1,
                     ye.reshape(N_EXPERTS, B, C, D), final_norm, C, final=(l == depth - 1), tm=tm_combine)
    return x
```

```python
import functools

import jax
import jax.numpy as jnp
from jax import lax
from jax.experimental import pallas as pl
from jax.experimental.pallas import tpu as pltpu

F32 = jnp.float32
BF16 = jnp.bfloat16
I32 = jnp.int32

HEAD_DIM = 64
LANES = 128
PAIRS = 4
GROUP = PAIRS * LANES
ATTN_SCALE = HEAD_DIM ** -0.5
ROPE_THETA = 10000.0
DIL_PATTERNS = ((128, 1), (512, 4), (2048, 16))
GRID_W = 64
NA_ROWS = 8
NA_COLS = 16
N_EXPERTS = 16
EC_CAPACITY = 2
EPS = 1e-6
NEG_INF = -1e30

VMEM_LIMIT = 56 * 1024 * 1024

_NT = (((1,), (1,)), ((), ()))


def _cparams(sem, vmem=VMEM_LIMIT):
    return pltpu.CompilerParams(dimension_semantics=sem, vmem_limit_bytes=vmem)


def _head0_mask():
    return lax.broadcasted_iota(I32, (1, LANES), 1) < HEAD_DIM


def _qkv_kernel(x_ref, g_ref, w_ref, cos_ref, sin_ref,
                qa_ref, ka_ref, va_ref, qb_ref, kb_ref, vb_ref):
    x = x_ref[...]
    ms = jnp.mean(x * x, axis=-1, keepdims=True)
    h = (x * lax.rsqrt(ms + EPS) * g_ref[...]).astype(BF16)
    cos = cos_ref[...]
    sin = sin_ref[...]
    lane = lax.broadcasted_iota(I32, (1, LANES), 1)
    first_half = (lane % HEAD_DIM) < HEAD_DIM // 2
    outs = (qa_ref, ka_ref, va_ref, qb_ref, kb_ref, vb_ref)
    for gi, o_ref in enumerate(outs):
        y = jnp.dot(h, w_ref[:, gi * GROUP:(gi + 1) * GROUP], preferred_element_type=F32)
        for p in range(PAIRS):
            s = y[:, p * LANES:(p + 1) * LANES]
            if gi < 2:
                partner = jnp.where(first_half, pltpu.roll(s, LANES - HEAD_DIM // 2, 1),
                                    pltpu.roll(s, HEAD_DIM // 2, 1))
                s = s * cos + partner * sin
            if gi in (0, 3):
                s = s * ATTN_SCALE
            o_ref[p] = s.astype(o_ref.dtype)


def _qkv(x, g, w_bf16, cos, sin, tm=512):
    B, T, D = x.shape
    tm = min(tm, T)
    slab = lambda dt: jax.ShapeDtypeStruct((B, PAIRS, T, LANES), dt)
    out_spec = pl.BlockSpec((None, PAIRS, tm, LANES), lambda b, i: (b, 0, i, 0))
    return pl.pallas_call(
        _qkv_kernel,
        grid=(B, T // tm),
        in_specs=[
            pl.BlockSpec((None, tm, D), lambda b, i: (b, i, 0)),
            pl.BlockSpec((1, D), lambda b, i: (0, 0)),
            pl.BlockSpec((D, 6 * GROUP), lambda b, i: (0, 0)),
            pl.BlockSpec((tm, LANES), lambda b, i: (i, 0)),
            pl.BlockSpec((tm, LANES), lambda b, i: (i, 0)),
        ],
        out_specs=[out_spec] * 6,
        out_shape=[slab(F32), slab(F32), slab(F32), slab(BF16), slab(BF16), slab(BF16)],
        compiler_params=_cparams(("parallel", "parallel")),
        name="qkv",
    )(x, g.reshape(1, D), w_bf16, cos, sin)


def _dil_kernel(q_ref, k_ref, v_ref, o_ref, m_ref, l_ref, bias_ref, *, T, Lq, unroll):
    head0 = _head0_mask()
    n_br = len(DIL_PATTERNS)
    for bi, (window, d) in enumerate(DIL_PATTERNS):
        half = window // (2 * d)
        L = T // d
        Wn = Lq + 2 * half
        nb = L // Lq
        rel0 = (lax.broadcasted_iota(I32, (Lq, Wn), 1) - lax.broadcasted_iota(I32, (Lq, Wn), 0))
        for ci, off in enumerate((0, -half, -2 * half)):
            bias_ref[ci] = jnp.where(jnp.abs(rel0 + off) <= half, 0.0, NEG_INF)

        def block(idx, carry, bi=bi, d=d, half=half, L=L, Wn=Wn, nb=nb):
            r = idx // nb
            n = idx % nb
            m0 = n * Lq
            ws = jnp.clip(m0 - half, 0, L - Wn)

            def rows(start, size):
                if d == 1:
                    return pl.ds(start, size)
                return pl.ds(r + d * start, size, stride=d)

            q = q_ref[rows(m0, Lq), :]
            k = k_ref[rows(ws, Wn), :].astype(BF16)
            v = v_ref[rows(ws, Wn), :].astype(BF16)
            bias = bias_ref[jnp.where(n == 0, 0, jnp.where(n == nb - 1, 2, 1))]
            q2 = jnp.concatenate([jnp.where(head0, q, 0.0), jnp.where(head0, 0.0, q)], axis=0).astype(BF16)
            s = lax.dot_general(q2, k, _NT, preferred_element_type=F32)
            stats = []
            for hs in (s[:Lq], s[Lq:]):
                hs = hs + bias
                mh = jnp.max(hs, axis=-1, keepdims=True)
                p = jnp.exp(hs - mh)
                stats.append((mh, jnp.sum(p, axis=-1, keepdims=True), p.astype(BF16)))
            pv = jnp.dot(jnp.concatenate([stats[0][2], stats[1][2]], axis=0), v, preferred_element_type=F32)
            mb = jnp.where(head0, stats[0][0], stats[1][0])
            lb = jnp.where(head0, stats[0][1], stats[1][1])
            ab = jnp.where(head0, pv[:Lq], pv[Lq:])
            qrows = rows(m0, Lq)
            if bi == 0:
                m_ref[qrows, :] = mb
                l_ref[qrows, :] = lb
                o_ref[qrows, :] = ab
            else:
                mo = m_ref[qrows, :]
                mn = jnp.maximum(mo, mb)
                al = jnp.exp(mo - mn)
                be = jnp.exp(mb - mn)
                ln = al * l_ref[qrows, :] + be * lb
                an = al * o_ref[qrows, :] + be * ab
                if bi == n_br - 1:
                    o_ref[qrows, :] = an / ln
                else:
                    m_ref[qrows, :] = mn
                    l_ref[qrows, :] = ln
                    o_ref[qrows, :] = an
            return carry

        lax.fori_loop(0, d * nb, block, 0, unroll=unroll)


def _dilated(qa, ka, va, Lq=128, unroll=4):
    B, P, T, _ = qa.shape
    halves = {window // (2 * d) for window, d in DIL_PATTERNS}
    assert len(halves) == 1, "the band-mask scratch is sized for one half-width"
    half = halves.pop()
    for window, d in DIL_PATTERNS:
        L = T // d
        assert T % d == 0 and L % Lq == 0 and L // Lq >= 2 and Lq >= half and Lq + 2 * half <= L, (T, window, d)
    spec = pl.BlockSpec((None, None, T, LANES), lambda b, p: (b, p, 0, 0))
    return pl.pallas_call(
        functools.partial(_dil_kernel, T=T, Lq=Lq, unroll=unroll),
        grid=(B, P),
        in_specs=[spec, spec, spec],
        out_specs=spec,
        out_shape=jax.ShapeDtypeStruct((B, P, T, LANES), F32),
        scratch_shapes=[pltpu.VMEM((T, LANES), F32), pltpu.VMEM((T, LANES), F32),
                        pltpu.VMEM((3, Lq, Lq + 2 * half), F32)],
        compiler_params=_cparams(("parallel", "parallel")),
        name="dilated",
    )(qa, ka, va)


def _na_bias_tables(rpb):
    H = rpb.shape[0]
    qc = jnp.arange(GRID_W)
    kc = jnp.arange(GRID_W)
    kr = jnp.arange(NA_ROWS)
    shift = jnp.arange(NA_ROWS)
    cs = jnp.clip(qc - NA_COLS // 2, 0, GRID_W - NA_COLS)
    valid = (kc[None, :] >= cs[:, None]) & (kc[None, :] < cs[:, None] + NA_COLS)
    dcol = jnp.clip(kc[None, :] - qc[:, None] + (NA_COLS - 1), 0, 2 * NA_COLS - 2)
    drow = kr[None, :] - shift[:, None] + (NA_ROWS - 1)
    row_pick = (drow[:, :, None] == jnp.arange(2 * NA_ROWS - 1)).astype(F32)
    col_pick = (dcol[:, :, None] == jnp.arange(2 * NA_COLS - 1)).astype(F32)
    hp = lax.Precision.HIGHEST
    t1 = jnp.einsum('hij,ski->hskj', rpb.astype(F32), row_pick, precision=hp)
    tbl = jnp.einsum('hskj,qcj->hsqkc', t1, col_pick, precision=hp)
    tbl = jnp.where(valid[None, None, :, None, :], tbl, NEG_INF)
    tbl = tbl.reshape(H // 2, 2, NA_ROWS, GRID_W, NA_ROWS * GRID_W).transpose(0, 2, 1, 3, 4)
    return tbl.reshape(H // 2, NA_ROWS, 2 * GRID_W, NA_ROWS * GRID_W)


def _na_kernel(q_ref, k_ref, v_ref, tbl_ref, o_ref, *, rows, unroll):
    head0 = _head0_mask()

    def body(r, carry):
        rs = jnp.clip(r - NA_ROWS // 2, 0, rows - NA_ROWS)
        shift = r - rs
        qrows = pl.ds(pl.multiple_of(r * GRID_W, GRID_W), GRID_W)
        kwin = pl.ds(pl.multiple_of(rs * GRID_W, GRID_W), NA_ROWS * GRID_W)
        q = q_ref[qrows, :]
        k = k_ref[kwin, :]
        v = v_ref[kwin, :]
        zero = jnp.zeros_like(q)
        q2 = jnp.concatenate([jnp.where(head0, q, zero), jnp.where(head0, zero, q)], axis=0)
        s = lax.dot_general(q2, k, _NT, preferred_element_type=F32) + tbl_ref[shift]
        m = jnp.max(s, axis=-1, keepdims=True)
        p = jnp.exp(s - m)
        l = jnp.sum(p, axis=-1, keepdims=True)
        o = jnp.dot(p.astype(BF16), v, preferred_element_type=F32) / l
        o_ref[qrows, :] = jnp.where(head0, o[:GRID_W], o[GRID_W:])
        return carry

    lax.fori_loop(0, rows, body, 0, unroll=unroll)


def _neighbourhood(qb, kb, vb, tbl, unroll=4):
    B, P, T, _ = qb.shape
    rows = T // GRID_W
    assert T % GRID_W == 0 and rows >= NA_ROWS
    spec = pl.BlockSpec((None, None, T, LANES), lambda b, p: (b, p, 0, 0))
    return pl.pallas_call(
        functools.partial(_na_kernel, rows=rows, unroll=unroll),
        grid=(B, P),
        in_specs=[spec, spec, spec,
                  pl.BlockSpec((None, NA_ROWS, 2 * GRID_W, NA_ROWS * GRID_W), lambda b, p: (p, 0, 0, 0))],
        out_specs=spec,
        out_shape=jax.ShapeDtypeStruct((B, P, T, LANES), F32),
        compiler_params=_cparams(("parallel", "parallel")),
        name="nbr",
    )(qb, kb, vb, tbl)


def _rms(a, g):
    ms = jnp.mean(a * a, axis=-1, keepdims=True)
    return a * lax.rsqrt(ms + EPS) * g


def _post_kernel(od_ref, on_ref, x_ref, wo_ref, gd_ref, gn_ref, gf_ref, wrh_ref, wrl_ref,
                 x1_ref, h_ref, aff_ref):
    dil = jnp.concatenate([od_ref[p] for p in range(PAIRS)], axis=-1)
    nbr = jnp.concatenate([on_ref[p] for p in range(PAIRS)], axis=-1)
    mixed = jnp.concatenate([_rms(dil, gd_ref[...]), _rms(nbr, gn_ref[...])], axis=-1).astype(BF16)
    x1 = x_ref[...] + jnp.dot(mixed, wo_ref[...], preferred_element_type=F32)
    x1_ref[...] = x1
    h = _rms(x1, gf_ref[...])
    hh = h.astype(BF16)
    hl = (h - hh.astype(F32)).astype(BF16)
    h_ref[...] = hh
    wrh = wrh_ref[...]
    lt = (lax.dot_general(wrh, hh, _NT, preferred_element_type=F32)
          + lax.dot_general(wrh, hl, _NT, preferred_element_type=F32)
          + lax.dot_general(wrl_ref[...], hh, _NT, preferred_element_type=F32))
    m = jnp.max(lt, axis=0, keepdims=True)
    e = jnp.exp(lt - m)
    aff_ref[...] = e / jnp.sum(e, axis=0, keepdims=True)


def _post(od, on, x, wo_bf16, gd, gn, gf, wr_hi, wr_lo, tm=512):
    B, T, D = x.shape
    tm = min(tm, T)
    E = wr_hi.shape[0]
    slab = pl.BlockSpec((None, PAIRS, tm, LANES), lambda b, i: (b, 0, i, 0))
    tok = pl.BlockSpec((None, tm, D), lambda b, i: (b, i, 0))
    full = lambda shape: pl.BlockSpec(shape, lambda b, i: (0,) * len(shape))
    return pl.pallas_call(
        _post_kernel,
        grid=(B, T // tm),
        in_specs=[slab, slab, tok, full((2 * GROUP, D)), full((1, GROUP)), full((1, GROUP)),
                  full((1, D)), full((E, D)), full((E, D))],
        out_specs=[tok, tok, pl.BlockSpec((None, E, tm), lambda b, i: (b, 0, i))],
        out_shape=[jax.ShapeDtypeStruct((B, T, D), F32), jax.ShapeDtypeStruct((B, T, D), BF16),
                   jax.ShapeDtypeStruct((B, E, T), F32)],
        compiler_params=_cparams(("parallel", "parallel")),
        name="post",
    )(od, on, x, wo_bf16, gd.reshape(1, GROUP), gn.reshape(1, GROUP), gf.reshape(1, D), wr_hi, wr_lo)


def _select_kernel(aff_ref, pos_ref, pref_ref, *, T, C):
    E = aff_ref.shape[0]
    bits = pltpu.bitcast(aff_ref[...], I32)
    t = jnp.zeros((E, 1), I32)
    for bit in range(30, -1, -1):
        cand = t | (1 << bit)
        cnt = jnp.sum(jnp.where(bits >= cand, 1.0, 0.0), axis=1, keepdims=True)
        t = jnp.where(cnt >= C, cand, t)
    gt = bits > t
    eq = bits == t
    need = C - jnp.sum(jnp.where(gt, 1.0, 0.0), axis=1, keepdims=True)
    tri = jnp.where(lax.broadcasted_iota(I32, (LANES, LANES), 0) < lax.broadcasted_iota(I32, (LANES, LANES), 1),
                    1.0, 0.0).astype(BF16)
    eq_carry = jnp.zeros((E, 1), F32)
    sel_carry = jnp.zeros((E, 1), F32)
    for c in range(T // LANES):
        sl = slice(c * LANES, (c + 1) * LANES)
        eqc = jnp.where(eq[:, sl], 1.0, 0.0)
        eq_rank = jnp.dot(eqc.astype(BF16), tri, preferred_element_type=F32) + eq_carry
        eq_carry = eq_carry + jnp.sum(eqc, axis=1, keepdims=True)
        sel = jnp.logical_or(gt[:, sl], jnp.logical_and(eq[:, sl], eq_rank < need))
        selc = jnp.where(sel, 1.0, 0.0)
        pref = (jnp.dot(selc.astype(BF16), tri, preferred_element_type=F32) + sel_carry).astype(I32)
        sel_carry = sel_carry + jnp.sum(selc, axis=1, keepdims=True)
        pref_ref[:, sl] = pref
        pos_ref[:, sl] = jnp.where(sel, pref, -1)


def _select(aff_t, C):
    B, E, T = aff_t.shape
    spec = pl.BlockSpec((None, E, T), lambda b: (b, 0, 0))
    return pl.pallas_call(
        functools.partial(_select_kernel, T=T, C=C),
        grid=(B,),
        in_specs=[spec],
        out_specs=[spec, spec],
        out_shape=[jax.ShapeDtypeStruct((B, E, T), I32), jax.ShapeDtypeStruct((B, E, T), I32)],
        compiler_params=_cparams(("parallel",)),
        name="select",
    )(aff_t)


def _count_below(ref, start, n, bound):
    lo = jnp.int32(0)
    step = 1 << (n.bit_length() - 1)
    while step:
        probe = lo + step
        ok = jnp.logical_and(probe <= n, ref[start + jnp.minimum(probe, n) - 1] < bound)
        lo = jnp.where(ok, probe, lo)
        step >>= 1
    return lo


def _dispatch_kernel(cc_ref, pos_ref, aff_ref, h_ref, xg_ref, gate_ref, acc_ref, gacc_ref, *, C, Cs, Tc, NC, KW):
    E = pl.num_programs(1)
    row = (pl.program_id(0) * E + pl.program_id(1)) * (NC + 1)
    for j in range(C // Cs):
        base = j * Cs

        c_lo = _count_below(cc_ref, row + 1, NC, base + 1)
        c_hi = _count_below(cc_ref, row, NC, base + Cs)
        slot = base + lax.broadcasted_iota(I32, (Cs, Tc), 0)
        c0 = jnp.minimum(c_lo, NC - KW)
        pw = pos_ref[pl.ds(c0, KW), :]
        aw = aff_ref[pl.ds(c0, KW), :]
        hits = [pw[kk:kk + 1, :] == slot for kk in range(KW)]
        onehot = jnp.concatenate([jnp.where(hit, 1.0, 0.0).astype(BF16) for hit in hits], axis=1)
        hwin = h_ref[pl.ds(pl.multiple_of(c0 * Tc, Tc), KW * Tc), :]
        acc_ref[...] = jnp.dot(onehot, hwin, preferred_element_type=F32)
        gsum = jnp.where(hits[0], aw[0:1, :], 0.0)
        for kk in range(1, KW):
            gsum = gsum + jnp.where(hits[kk], aw[kk:kk + 1, :], 0.0)
        gacc_ref[...] = jnp.sum(gsum, axis=1, keepdims=True)

        def chunk(c, carry, slot=slot):
            hit = pos_ref[pl.ds(c, 1), :] == slot
            hc = h_ref[pl.ds(pl.multiple_of(c * Tc, Tc), Tc), :]
            acc_ref[...] += jnp.dot(jnp.where(hit, 1.0, 0.0).astype(BF16), hc, preferred_element_type=F32)
            gacc_ref[...] += jnp.sum(jnp.where(hit, aff_ref[pl.ds(c, 1), :], 0.0), axis=1, keepdims=True)
            return carry

        lax.fori_loop(c0 + KW, c_hi, chunk, 0)
        xg_ref[base:base + Cs, :] = acc_ref[...].astype(BF16)
        gate_ref[base:base + Cs, :] = jnp.broadcast_to(gacc_ref[...], (Cs, LANES))


def _dispatch(cc, pos, aff_t, h, C, Cs=128, Tc=LANES, KW=12):
    B, E, T = pos.shape
    D = h.shape[-1]
    NC = T // Tc
    assert T % Tc == 0 and C % Cs == 0 and KW <= NC
    row_spec = pl.BlockSpec((None, None, NC, Tc), lambda b, e, cc: (b, e, 0, 0))
    return pl.pallas_call(
        functools.partial(_dispatch_kernel, C=C, Cs=Cs, Tc=Tc, NC=NC, KW=KW),
        grid_spec=pltpu.PrefetchScalarGridSpec(
            num_scalar_prefetch=1,
            grid=(B, E),
            in_specs=[row_spec, row_spec,
                      pl.BlockSpec((None, T, D), lambda b, e, cc: (b, 0, 0), pipeline_mode=pl.Buffered(1))],
            out_specs=[pl.BlockSpec((None, None, C, D), lambda b, e, cc: (e, b, 0, 0)),
                       pl.BlockSpec((None, None, C, LANES), lambda b, e, cc: (e, b, 0, 0))],
            scratch_shapes=[pltpu.VMEM((Cs, D), F32), pltpu.VMEM((Cs, 1), F32)],
        ),
        out_shape=[jax.ShapeDtypeStruct((E, B, C, D), BF16), jax.ShapeDtypeStruct((E, B, C, LANES), F32)],
        compiler_params=_cparams(("arbitrary", "arbitrary")),
        name="dispatch",
    )(cc, pos.reshape(B, E, NC, Tc), aff_t.reshape(B, E, NC, Tc), h)


def _ffn_kernel(x_ref, wg_ref, wu_ref, wd_ref, gate_ref, y_ref, acc_ref, *, FF, fc):
    x = x_ref[...]
    for kc in range(FF // fc):
        sl = slice(kc * fc, (kc + 1) * fc)
        g = jnp.dot(x, wg_ref[:, sl], preferred_element_type=F32)
        u = jnp.dot(x, wu_ref[:, sl], preferred_element_type=F32)
        a = (g * jax.nn.sigmoid(g) * u).astype(BF16)
        y = jnp.dot(a, wd_ref[sl, :], preferred_element_type=F32)
        if kc == 0:
            acc_ref[...] = y
        else:
            acc_ref[...] += y
    gate = gate_ref[...]
    for s in range(y_ref.shape[-1] // LANES):
        sl = slice(s * LANES, (s + 1) * LANES)
        y_ref[:, sl] = (acc_ref[:, sl] * gate).astype(y_ref.dtype)


def _ffn(xg, wg, wu, wd, gates, tm=512, fc=256):
    E, R, D = xg.shape
    FF = wg.shape[-1]
    assert R % tm == 0 and FF % fc == 0
    return pl.pallas_call(
        functools.partial(_ffn_kernel, FF=FF, fc=fc),
        grid=(E, R // tm),
        in_specs=[pl.BlockSpec((None, tm, D), lambda e, i: (e, i, 0)),
                  pl.BlockSpec((None, D, FF), lambda e, i: (e, 0, 0)),
                  pl.BlockSpec((None, D, FF), lambda e, i: (e, 0, 0)),
                  pl.BlockSpec((None, FF, D), lambda e, i: (e, 0, 0)),
                  pl.BlockSpec((None, tm, LANES), lambda e, i: (e, i, 0))],
        out_specs=pl.BlockSpec((None, tm, D), lambda e, i: (e, i, 0)),
        out_shape=jax.ShapeDtypeStruct((E, R, D), BF16),
        scratch_shapes=[pltpu.VMEM((tm, D), F32)],
        compiler_params=_cparams(("parallel", "parallel")),
        name="ffn",
    )(xg, wg, wu, wd, gates)


def _combine_kernel(ss_ref, pos_ref, x1_ref, ye_ref, gfin_ref, out_ref, acc_ref, yall_ref, *, C, W, NT, final):
    E = ye_ref.shape[0]
    tm = x1_ref.shape[0]
    b = pl.program_id(0)
    i = pl.program_id(1)
    starts = []
    for e in range(E):
        row = (b * E + e) * (NT + 1) + i
        s0 = ss_ref[row]
        s1 = ss_ref[row + 1]
        a = pl.multiple_of(jnp.minimum((s0 // 16) * 16, C - W), 16)
        yall_ref[e * W:(e + 1) * W, :] = ye_ref[e, pl.ds(a, W), :]
        starts.append((a, s1))
    lane = lax.broadcasted_iota(I32, (tm, 2 * W), 1)
    first = lane < W
    pieces = []
    for e in range(0, E, 2):
        slot = jnp.where(first, starts[e][0] + lane, starts[e + 1][0] + lane - W)
        pcol = jnp.where(first, pos_ref[:, e:e + 1], pos_ref[:, e + 1:e + 2])
        pieces.append(jnp.where(pcol == slot, 1.0, 0.0).astype(BF16))
    onehot = jnp.concatenate(pieces, axis=1)
    acc_ref[...] = x1_ref[...] + jnp.dot(onehot, yall_ref[...], preferred_element_type=F32)
    lane_w = lax.broadcasted_iota(I32, (tm, W), 1)
    for e in range(E):
        a0, s1 = starts[e]
        n_extra = jnp.maximum((s1 - a0 - 1) // W, 0)
        pcol = pos_ref[:, e:e + 1]

        def chunk(k, carry, e=e, a0=a0, pcol=pcol):
            lo = a0 + k * W
            a = pl.multiple_of(jnp.minimum(lo, C - W), 16)
            slot = a + lane_w
            hit = jnp.logical_and(pcol == slot, slot >= lo)
            y = ye_ref[e, pl.ds(a, W), :]
            acc_ref[...] += jnp.dot(jnp.where(hit, 1.0, 0.0).astype(BF16), y, preferred_element_type=F32)
            return carry

        lax.fori_loop(1, n_extra + 1, chunk, 0)
    out = acc_ref[...]
    if final:
        out = _rms(out, gfin_ref[...])
    out_ref[...] = out


def _combine(ss, pos_t, x1, ye, gfin, C, final, tm=256, W=64):
    B, T, D = x1.shape
    E = ye.shape[0]
    tm = min(tm, T)
    NT = T // tm
    assert C % 16 == 0 and W % 16 == 0 and W <= C
    return pl.pallas_call(
        functools.partial(_combine_kernel, C=C, W=W, NT=NT, final=final),
        grid_spec=pltpu.PrefetchScalarGridSpec(
            num_scalar_prefetch=1,
            grid=(B, NT),
            in_specs=[pl.BlockSpec((None, tm, E), lambda b, i, ss: (b, i, 0)),
                      pl.BlockSpec((None, tm, D), lambda b, i, ss: (b, i, 0)),
                      pl.BlockSpec((E, None, C, D), lambda b, i, ss: (0, b, 0, 0), pipeline_mode=pl.Buffered(1)),
                      pl.BlockSpec((1, D), lambda b, i, ss: (0, 0))],
            out_specs=pl.BlockSpec((None, tm, D), lambda b, i, ss: (b, i, 0)),
            scratch_shapes=[pltpu.VMEM((tm, D), F32), pltpu.VMEM((E * W, D), BF16)],
        ),
        out_shape=jax.ShapeDtypeStruct((B, T, D), F32),
        compiler_params=_cparams(("arbitrary", "arbitrary")),
        name="combine",
    )(ss, pos_t, x1, ye, gfin.reshape(1, D))


def _rope_tables(T):
    pos = jnp.arange(T, dtype=F32)
    inv = ROPE_THETA ** (-jnp.arange(0, HEAD_DIM, 2, dtype=F32) / HEAD_DIM)
    ang = pos[:, None] * inv[None, :]
    cos = jnp.tile(jnp.cos(ang), (1, LANES // (HEAD_DIM // 2)))
    sin = jnp.sin(ang)
    sin = jnp.tile(jnp.concatenate([-sin, sin], axis=-1), (1, LANES // HEAD_DIM))
    return cos, sin


def _chunk_starts(pref, step, total):
    B, E, _ = pref.shape
    tail = jnp.full((B, E, 1), total, I32)
    return jnp.concatenate([pref[:, :, ::step], tail], axis=-1).reshape(-1)


def kernel(x, attn_norm, w_in, dil_out_norm, na_out_norm, na_rpb, w_out, ffn_norm, w_router,
           w_gate, w_up, w_down, final_norm):
    B, T, D = x.shape
    depth = w_in.shape[0]
    C = EC_CAPACITY * T // N_EXPERTS
    cos, sin = _rope_tables(T)
    tc_dispatch, tm_combine = LANES, min(256, T)
    for l in range(depth):
        qa, ka, va, qb, kb, vb = _qkv(x, attn_norm[l], w_in[l].astype(BF16), cos, sin)
        od = _dilated(qa, ka, va)
        on = _neighbourhood(qb, kb, vb, _na_bias_tables(na_rpb[l]))
        wr_t = w_router[l].T
        wr_hi = wr_t.astype(BF16)
        wr_lo = (wr_t - wr_hi.astype(F32)).astype(BF16)
        x1, h, aff_t = _post(od, on, x, w_out[l].astype(BF16), dil_out_norm[l], na_out_norm[l],
                             ffn_norm[l], wr_hi, wr_lo)
        pos, pref = _select(aff_t, C)
        xg, gates = _dispatch(_chunk_starts(pref, tc_dispatch, C), pos, aff_t, h, C, Tc=tc_dispatch)
        ye = _ffn(xg.reshape(N_EXPERTS, B * C, D), w_gate[l].astype(BF16), w_up[l].astype(BF16),
                  w_down[l].astype(BF16), gates.reshape(N_EXPERTS, B * C, LANES))
        x = _combine(_chunk_starts(pref, tm_combine, C), pos.transpose(0, 2, 1), x1,
                     ye.reshape(N_EXPERTS, B, C, D), final_norm, C, final=(l == depth - 1), tm=tm_combine)
    return x
```

```python
import functools

import jax
import jax.numpy as jnp
from jax import lax
from jax.experimental import pallas as pl
from jax.experimental.pallas import tpu as pltpu

F32 = jnp.float32
BF16 = jnp.bfloat16
I32 = jnp.int32

HEAD_DIM = 64
LANES = 128
PAIRS = 4
GROUP = PAIRS * LANES
ATTN_SCALE = HEAD_DIM ** -0.5
ROPE_THETA = 10000.0
DIL_PATTERNS = ((128, 1), (512, 4), (2048, 16))
GRID_W = 64
NA_ROWS = 8
NA_COLS = 16
N_EXPERTS = 16
EC_CAPACITY = 2
EPS = 1e-6
NEG_INF = -1e30

VMEM_LIMIT = 56 * 1024 * 1024

_NT = (((1,), (1,)), ((), ()))


def _cparams(sem, vmem=VMEM_LIMIT):
    return pltpu.CompilerParams(dimension_semantics=sem, vmem_limit_bytes=vmem)


def _head0_mask():
    return lax.broadcasted_iota(I32, (1, LANES), 1) < HEAD_DIM


def _clip(x, lo, hi):
    return min(max(x, lo), hi) if isinstance(x, int) else jnp.clip(x, lo, hi)


def _aligned(x, m):
    return x if isinstance(x, int) else pl.multiple_of(x, m)


def _qkv_kernel(x_ref, g_ref, w_ref, cos_ref, sin_ref,
                qa_ref, ka_ref, va_ref, qb_ref, kb_ref, vb_ref):
    x = x_ref[...]
    ms = jnp.mean(x * x, axis=-1, keepdims=True)
    h = (x * lax.rsqrt(ms + EPS) * g_ref[...]).astype(BF16)
    cos = cos_ref[...]
    sin = sin_ref[...]
    lane = lax.broadcasted_iota(I32, (1, LANES), 1)
    first_half = (lane % HEAD_DIM) < HEAD_DIM // 2
    outs = (qa_ref, ka_ref, va_ref, qb_ref, kb_ref, vb_ref)
    for gi, o_ref in enumerate(outs):
        y = jnp.dot(h, w_ref[:, gi * GROUP:(gi + 1) * GROUP], preferred_element_type=F32)
        for p in range(PAIRS):
            s = y[:, p * LANES:(p + 1) * LANES]
            if gi < 2:
                partner = jnp.where(first_half, pltpu.roll(s, LANES - HEAD_DIM // 2, 1),
                                    pltpu.roll(s, HEAD_DIM // 2, 1))
                s = s * cos + partner * sin
            if gi in (0, 3):
                s = s * ATTN_SCALE
            o_ref[p] = s.astype(o_ref.dtype)


def _qkv(x, g, w_bf16, cos, sin, tm=512):
    B, T, D = x.shape
    tm = min(tm, T)
    slab = lambda dt: jax.ShapeDtypeStruct((B, PAIRS, T, LANES), dt)
    out_spec = pl.BlockSpec((None, PAIRS, tm, LANES), lambda b, i: (b, 0, i, 0))
    return pl.pallas_call(
        _qkv_kernel,
        grid=(B, T // tm),
        in_specs=[
            pl.BlockSpec((None, tm, D), lambda b, i: (b, i, 0)),
            pl.BlockSpec((1, D), lambda b, i: (0, 0)),
            pl.BlockSpec((D, 6 * GROUP), lambda b, i: (0, 0)),
            pl.BlockSpec((tm, LANES), lambda b, i: (i, 0)),
            pl.BlockSpec((tm, LANES), lambda b, i: (i, 0)),
        ],
        out_specs=[out_spec] * 6,
        out_shape=[slab(F32), slab(F32), slab(F32), slab(BF16), slab(BF16), slab(BF16)],
        compiler_params=_cparams(("parallel", "parallel")),
        name="qkv",
    )(x, g.reshape(1, D), w_bf16, cos, sin)


def _dil_kernel(q_ref, k_ref, v_ref, o_ref, m_ref, l_ref, bias_ref, s_buf, p_buf, st_buf, *, T, Lq):
    head0 = _head0_mask()
    n_br = len(DIL_PATTERNS)
    for bi, (window, d) in enumerate(DIL_PATTERNS):
        half = window // (2 * d)
        L = T // d
        Wn = Lq + 2 * half
        nb = L // Lq
        rel0 = (lax.broadcasted_iota(I32, (Lq, Wn), 1) - lax.broadcasted_iota(I32, (Lq, Wn), 0))
        for ci, off in enumerate((0, -half, -2 * half)):
            bias_ref[ci] = jnp.where(jnp.abs(rel0 + off) <= half, 0.0, NEG_INF)

        def geom(idx, d=d, half=half, L=L, Wn=Wn, nb=nb):
            r = idx // nb
            n = idx % nb
            m0 = n * Lq
            ws = _clip(m0 - half, 0, L - Wn)

            def rows(start, size):
                if d == 1:
                    return pl.ds(start, size)
                return pl.ds(r + d * start, size, stride=d)

            return n, rows(m0, Lq), rows(ws, Wn)

        def stage_a(idx, geom=geom, nb=nb):
            n, qrows, kvrows = geom(idx)
            q = q_ref[qrows, :]
            k = k_ref[kvrows, :].astype(BF16)
            if isinstance(n, int):
                case = 0 if n == 0 else (2 if n == nb - 1 else 1)
            else:
                case = jnp.where(n == 0, 0, jnp.where(n == nb - 1, 2, 1))
            bias = bias_ref[case]
            q2 = jnp.concatenate([jnp.where(head0, q, 0.0), jnp.where(head0, 0.0, q)], axis=0).astype(BF16)
            s = lax.dot_general(q2, k, _NT, preferred_element_type=F32)
            s_buf[idx % 2, :Lq] = s[:Lq] + bias
            s_buf[idx % 2, Lq:] = s[Lq:] + bias

        def stage_b(idx):
            stats = []
            for rows_h in (slice(0, Lq), slice(Lq, 2 * Lq)):
                hs = s_buf[idx % 2, rows_h]
                mh = jnp.max(hs, axis=-1, keepdims=True)
                p = jnp.exp(hs - mh)
                p_buf[idx % 2, rows_h] = p.astype(BF16)
                stats.append((mh, jnp.sum(p, axis=-1, keepdims=True)))
            st_buf[idx % 2, 0] = jnp.where(head0, stats[0][0], stats[1][0])
            st_buf[idx % 2, 1] = jnp.where(head0, stats[0][1], stats[1][1])

        def stage_c(idx, bi=bi, geom=geom):
            _, qrows, kvrows = geom(idx)
            v = v_ref[kvrows, :].astype(BF16)
            pv = jnp.dot(p_buf[idx % 2], v, preferred_element_type=F32)
            mb = st_buf[idx % 2, 0]
            lb = st_buf[idx % 2, 1]
            ab = jnp.where(head0, pv[:Lq], pv[Lq:])
            if bi == 0:
                m_ref[qrows, :] = mb
                l_ref[qrows, :] = lb
                o_ref[qrows, :] = ab
            else:
                mo = m_ref[qrows, :]
                mn = jnp.maximum(mo, mb)
                al = jnp.exp(mo - mn)
                be = jnp.exp(mb - mn)
                ln = al * l_ref[qrows, :] + be * lb
                an = al * o_ref[qrows, :] + be * ab
                if bi == n_br - 1:
                    o_ref[qrows, :] = an / ln
                else:
                    m_ref[qrows, :] = mn
                    l_ref[qrows, :] = ln
                    o_ref[qrows, :] = an

        n_blk = d * nb
        stage_a(0)
        stage_a(1)
        stage_b(0)

        def body(i, carry, stage_a=stage_a, stage_b=stage_b, stage_c=stage_c):
            stage_c(i)
            stage_b(i + 1)
            stage_a(i + 2)
            return carry

        lax.fori_loop(0, n_blk - 2, body, 0, unroll=2)
        stage_c(n_blk - 2)
        stage_b(n_blk - 1)
        stage_c(n_blk - 1)


def _dilated(qa, ka, va, Lq=128):
    B, P, T, _ = qa.shape
    halves = {window // (2 * d) for window, d in DIL_PATTERNS}
    assert len(halves) == 1, "the band-mask scratch is sized for one half-width"
    half = halves.pop()
    for window, d in DIL_PATTERNS:
        L = T // d
        assert T % d == 0 and L % Lq == 0 and L // Lq >= 2 and Lq >= half and Lq + 2 * half <= L, (T, window, d)
    spec = pl.BlockSpec((None, None, T, LANES), lambda b, p: (b, p, 0, 0))
    return pl.pallas_call(
        functools.partial(_dil_kernel, T=T, Lq=Lq),
        grid=(B, P),
        in_specs=[spec, spec, spec],
        out_specs=spec,
        out_shape=jax.ShapeDtypeStruct((B, P, T, LANES), F32),
        scratch_shapes=[pltpu.VMEM((T, LANES), F32), pltpu.VMEM((T, LANES), F32),
                        pltpu.VMEM((3, Lq, Lq + 2 * half), F32),
                        pltpu.VMEM((2, 2 * Lq, Lq + 2 * half), F32),
                        pltpu.VMEM((2, 2 * Lq, Lq + 2 * half), BF16),
                        pltpu.VMEM((2, 2, Lq, LANES), F32)],
        compiler_params=_cparams(("parallel", "parallel")),
        name="dilated",
    )(qa, ka, va)


def _na_bias_tables(rpb):
    H = rpb.shape[0]
    qc = jnp.arange(GRID_W)
    kc = jnp.arange(GRID_W)
    kr = jnp.arange(NA_ROWS)
    shift = jnp.arange(NA_ROWS)
    cs = jnp.clip(qc - NA_COLS // 2, 0, GRID_W - NA_COLS)
    valid = (kc[None, :] >= cs[:, None]) & (kc[None, :] < cs[:, None] + NA_COLS)
    dcol = jnp.clip(kc[None, :] - qc[:, None] + (NA_COLS - 1), 0, 2 * NA_COLS - 2)
    drow = kr[None, :] - shift[:, None] + (NA_ROWS - 1)
    row_pick = (drow[:, :, None] == jnp.arange(2 * NA_ROWS - 1)).astype(F32)
    col_pick = (dcol[:, :, None] == jnp.arange(2 * NA_COLS - 1)).astype(F32)
    hp = lax.Precision.HIGHEST
    t1 = jnp.einsum('hij,ski->hskj', rpb.astype(F32), row_pick, precision=hp)
    tbl = jnp.einsum('hskj,qcj->hsqkc', t1, col_pick, precision=hp)
    tbl = jnp.where(valid[None, None, :, None, :], tbl, NEG_INF)
    tbl = tbl.reshape(H // 2, 2, NA_ROWS, GRID_W, NA_ROWS * GRID_W).transpose(0, 2, 1, 3, 4)
    return tbl.reshape(H // 2, NA_ROWS, 2 * GRID_W, NA_ROWS * GRID_W)


def _na_kernel(q_ref, k_ref, v_ref, tbl_ref, o_ref, s_buf, p_buf, *, rows, rpu):
    head0 = _head0_mask()
    n_units = rows // rpu

    def geom(r):
        rs = _clip(r - NA_ROWS // 2, 0, rows - NA_ROWS)
        qrows = pl.ds(_aligned(r * GRID_W, GRID_W), GRID_W)
        kwin = pl.ds(_aligned(rs * GRID_W, GRID_W), NA_ROWS * GRID_W)
        return r - rs, qrows, kwin

    def stage_a(u):
        for t in range(rpu):
            shift, qrows, kwin = geom(u * rpu + t)
            q = q_ref[qrows, :]
            zero = jnp.zeros_like(q)
            q2 = jnp.concatenate([jnp.where(head0, q, zero), jnp.where(head0, zero, q)], axis=0)
            s_buf[u % 2, t] = lax.dot_general(q2, k_ref[kwin, :], _NT, preferred_element_type=F32) + tbl_ref[shift]

    def stage_b(u):
        for t in range(rpu):
            s = s_buf[u % 2, t]
            p = jnp.exp(s - jnp.max(s, axis=-1, keepdims=True))
            p_buf[u % 2, t] = (p * (1.0 / jnp.sum(p, axis=-1, keepdims=True))).astype(BF16)

    def stage_c(u):
        for t in range(rpu):
            _, qrows, kwin = geom(u * rpu + t)
            o = jnp.dot(p_buf[u % 2, t], v_ref[kwin, :], preferred_element_type=F32)
            o_ref[qrows, :] = jnp.where(head0, o[:GRID_W], o[GRID_W:])

    stage_a(0)
    stage_a(1)
    stage_b(0)

    def body(u, carry):
        stage_c(u)
        stage_b(u + 1)
        stage_a(u + 2)
        return carry

    lax.fori_loop(0, n_units - 2, body, 0)
    stage_c(n_units - 2)
    stage_b(n_units - 1)
    stage_c(n_units - 1)


def _neighbourhood(qb, kb, vb, tbl, rpu=2):
    B, P, T, _ = qb.shape
    rows = T // GRID_W
    assert T % GRID_W == 0 and rows >= NA_ROWS and rows % rpu == 0 and rows // rpu >= 3
    spec = pl.BlockSpec((None, None, T, LANES), lambda b, p: (b, p, 0, 0))
    return pl.pallas_call(
        functools.partial(_na_kernel, rows=rows, rpu=rpu),
        grid=(B, P),
        in_specs=[spec, spec, spec,
                  pl.BlockSpec((None, NA_ROWS, 2 * GRID_W, NA_ROWS * GRID_W), lambda b, p: (p, 0, 0, 0))],
        out_specs=spec,
        out_shape=jax.ShapeDtypeStruct((B, P, T, LANES), F32),
        scratch_shapes=[pltpu.VMEM((2, rpu, 2 * GRID_W, NA_ROWS * GRID_W), F32),
                        pltpu.VMEM((2, rpu, 2 * GRID_W, NA_ROWS * GRID_W), BF16)],
        compiler_params=_cparams(("parallel", "parallel")),
        name="nbr",
    )(qb, kb, vb, tbl)


def _rms(a, g):
    ms = jnp.mean(a * a, axis=-1, keepdims=True)
    return a * lax.rsqrt(ms + EPS) * g


def _post_kernel(od_ref, on_ref, x_ref, wo_ref, gd_ref, gn_ref, gf_ref, wrh_ref, wrl_ref,
                 x1_ref, h_ref, aff_ref):
    dil = jnp.concatenate([od_ref[p] for p in range(PAIRS)], axis=-1)
    nbr = jnp.concatenate([on_ref[p] for p in range(PAIRS)], axis=-1)
    mixed = jnp.concatenate([_rms(dil, gd_ref[...]), _rms(nbr, gn_ref[...])], axis=-1).astype(BF16)
    x1 = x_ref[...] + jnp.dot(mixed, wo_ref[...], preferred_element_type=F32)
    x1_ref[...] = x1
    h = _rms(x1, gf_ref[...])
    hh = h.astype(BF16)
    hl = (h - hh.astype(F32)).astype(BF16)
    h_ref[...] = hh
    wrh = wrh_ref[...]
    lt = (lax.dot_general(wrh, hh, _NT, preferred_element_type=F32)
          + lax.dot_general(wrh, hl, _NT, preferred_element_type=F32)
          + lax.dot_general(wrl_ref[...], hh, _NT, preferred_element_type=F32))
    m = jnp.max(lt, axis=0, keepdims=True)
    e = jnp.exp(lt - m)
    aff_ref[...] = e / jnp.sum(e, axis=0, keepdims=True)


def _post(od, on, x, wo_bf16, gd, gn, gf, wr_hi, wr_lo, tm=512):
    B, T, D = x.shape
    tm = min(tm, T)
    E = wr_hi.shape[0]
    slab = pl.BlockSpec((None, PAIRS, tm, LANES), lambda b, i: (b, 0, i, 0))
    tok = pl.BlockSpec((None, tm, D), lambda b, i: (b, i, 0))
    full = lambda shape: pl.BlockSpec(shape, lambda b, i: (0,) * len(shape))
    return pl.pallas_call(
        _post_kernel,
        grid=(B, T // tm),
        in_specs=[slab, slab, tok, full((2 * GROUP, D)), full((1, GROUP)), full((1, GROUP)),
                  full((1, D)), full((E, D)), full((E, D))],
        out_specs=[tok, tok, pl.BlockSpec((None, E, tm), lambda b, i: (b, 0, i))],
        out_shape=[jax.ShapeDtypeStruct((B, T, D), F32), jax.ShapeDtypeStruct((B, T, D), BF16),
                   jax.ShapeDtypeStruct((B, E, T), F32)],
        compiler_params=_cparams(("parallel", "parallel")),
        name="post",
    )(od, on, x, wo_bf16, gd.reshape(1, GROUP), gn.reshape(1, GROUP), gf.reshape(1, D), wr_hi, wr_lo)


def _select_kernel(aff_ref, pos_ref, pref_ref, *, T, C):
    E = aff_ref.shape[0]
    bits = pltpu.bitcast(aff_ref[...], I32)
    t = jnp.zeros((E, 1), I32)
    for bit in range(30, -1, -1):
        cand = t | (1 << bit)
        cnt = jnp.sum(jnp.where(bits >= cand, 1.0, 0.0), axis=1, keepdims=True)
        t = jnp.where(cnt >= C, cand, t)
    gt = bits > t
    eq = bits == t
    need = C - jnp.sum(jnp.where(gt, 1.0, 0.0), axis=1, keepdims=True)
    tri = jnp.where(lax.broadcasted_iota(I32, (LANES, LANES), 0) < lax.broadcasted_iota(I32, (LANES, LANES), 1),
                    1.0, 0.0).astype(BF16)
    eq_carry = jnp.zeros((E, 1), F32)
    sel_carry = jnp.zeros((E, 1), F32)
    for c in range(T // LANES):
        sl = slice(c * LANES, (c + 1) * LANES)
        eqc = jnp.where(eq[:, sl], 1.0, 0.0)
        eq_rank = jnp.dot(eqc.astype(BF16), tri, preferred_element_type=F32) + eq_carry
        eq_carry = eq_carry + jnp.sum(eqc, axis=1, keepdims=True)
        sel = jnp.logical_or(gt[:, sl], jnp.logical_and(eq[:, sl], eq_rank < need))
        selc = jnp.where(sel, 1.0, 0.0)
        pref = (jnp.dot(selc.astype(BF16), tri, preferred_element_type=F32) + sel_carry).astype(I32)
        sel_carry = sel_carry + jnp.sum(selc, axis=1, keepdims=True)
        pref_ref[:, sl] = pref
        pos_ref[:, sl] = jnp.where(sel, pref, -1)


def _select(aff_t, C):
    B, E, T = aff_t.shape
    spec = pl.BlockSpec((None, E, T), lambda b: (b, 0, 0))
    return pl.pallas_call(
        functools.partial(_select_kernel, T=T, C=C),
        grid=(B,),
        in_specs=[spec],
        out_specs=[spec, spec],
        out_shape=[jax.ShapeDtypeStruct((B, E, T), I32), jax.ShapeDtypeStruct((B, E, T), I32)],
        compiler_params=_cparams(("parallel",)),
        name="select",
    )(aff_t)


def _count_below(ref, start, n, bound):
    lo = jnp.int32(0)
    step = 1 << (n.bit_length() - 1)
    while step:
        probe = lo + step
        ok = jnp.logical_and(probe <= n, ref[start + jnp.minimum(probe, n) - 1] < bound)
        lo = jnp.where(ok, probe, lo)
        step >>= 1
    return lo


def _dispatch_kernel(cc_ref, pos_ref, aff_ref, h_ref, xg_ref, gate_ref, acc_ref, gacc_ref, *, C, Cs, Tc, NC, KW):
    E = pl.num_programs(1)
    row = (pl.program_id(0) * E + pl.program_id(1)) * (NC + 1)
    for j in range(C // Cs):
        base = j * Cs

        c_lo = _count_below(cc_ref, row + 1, NC, base + 1)
        c_hi = _count_below(cc_ref, row, NC, base + Cs)
        slot = base + lax.broadcasted_iota(I32, (Cs, Tc), 0)
        c0 = jnp.minimum(c_lo, NC - KW)
        pw = pos_ref[pl.ds(c0, KW), :]
        aw = aff_ref[pl.ds(c0, KW), :]
        hits = [pw[kk:kk + 1, :] == slot for kk in range(KW)]
        onehot = jnp.concatenate([jnp.where(hit, 1.0, 0.0).astype(BF16) for hit in hits], axis=1)
        hwin = h_ref[pl.ds(pl.multiple_of(c0 * Tc, Tc), KW * Tc), :]
        acc_ref[...] = jnp.dot(onehot, hwin, preferred_element_type=F32)
        gsum = jnp.where(hits[0], aw[0:1, :], 0.0)
        for kk in range(1, KW):
            gsum = gsum + jnp.where(hits[kk], aw[kk:kk + 1, :], 0.0)
        gacc_ref[...] = jnp.sum(gsum, axis=1, keepdims=True)

        def chunk(c, carry, slot=slot):
            hit = pos_ref[pl.ds(c, 1), :] == slot
            hc = h_ref[pl.ds(pl.multiple_of(c * Tc, Tc), Tc), :]
            acc_ref[...] += jnp.dot(jnp.where(hit, 1.0, 0.0).astype(BF16), hc, preferred_element_type=F32)
            gacc_ref[...] += jnp.sum(jnp.where(hit, aff_ref[pl.ds(c, 1), :], 0.0), axis=1, keepdims=True)
            return carry

        lax.fori_loop(c0 + KW, c_hi, chunk, 0)
        xg_ref[base:base + Cs, :] = acc_ref[...].astype(BF16)
        gate_ref[base:base + Cs, :] = jnp.broadcast_to(gacc_ref[...], (Cs, LANES))


def _dispatch(cc, pos, aff_t, h, C, Cs=128, Tc=LANES, KW=12):
    B, E, T = pos.shape
    D = h.shape[-1]
    NC = T // Tc
    assert T % Tc == 0 and C % Cs == 0 and KW <= NC
    row_spec = pl.BlockSpec((None, None, NC, Tc), lambda b, e, cc: (b, e, 0, 0))
    return pl.pallas_call(
        functools.partial(_dispatch_kernel, C=C, Cs=Cs, Tc=Tc, NC=NC, KW=KW),
        grid_spec=pltpu.PrefetchScalarGridSpec(
            num_scalar_prefetch=1,
            grid=(B, E),
            in_specs=[row_spec, row_spec,
                      pl.BlockSpec((None, T, D), lambda b, e, cc: (b, 0, 0), pipeline_mode=pl.Buffered(1))],
            out_specs=[pl.BlockSpec((None, None, C, D), lambda b, e, cc: (e, b, 0, 0)),
                       pl.BlockSpec((None, None, C, LANES), lambda b, e, cc: (e, b, 0, 0))],
            scratch_shapes=[pltpu.VMEM((Cs, D), F32), pltpu.VMEM((Cs, 1), F32)],
        ),
        out_shape=[jax.ShapeDtypeStruct((E, B, C, D), BF16), jax.ShapeDtypeStruct((E, B, C, LANES), F32)],
        compiler_params=_cparams(("arbitrary", "arbitrary")),
        name="dispatch",
    )(cc, pos.reshape(B, E, NC, Tc), aff_t.reshape(B, E, NC, Tc), h)


def _ffn_kernel(x_ref, wg_ref, wu_ref, wd_ref, gate_ref, y_ref, acc_ref, *, FF, fc):
    x = x_ref[...]
    for kc in range(FF // fc):
        sl = slice(kc * fc, (kc + 1) * fc)
        g = jnp.dot(x, wg_ref[:, sl], preferred_element_type=F32)
        u = jnp.dot(x, wu_ref[:, sl], preferred_element_type=F32)
        a = (g * jax.nn.sigmoid(g) * u).astype(BF16)
        y = jnp.dot(a, wd_ref[sl, :], preferred_element_type=F32)
        if kc == 0:
            acc_ref[...] = y
        else:
            acc_ref[...] += y
    gate = gate_ref[...]
    for s in range(y_ref.shape[-1] // LANES):
        sl = slice(s * LANES, (s + 1) * LANES)
        y_ref[:, sl] = (acc_ref[:, sl] * gate).astype(y_ref.dtype)


def _ffn(xg, wg, wu, wd, gates, tm=512, fc=256):
    E, R, D = xg.shape
    FF = wg.shape[-1]
    assert R % tm == 0 and FF % fc == 0
    return pl.pallas_call(
        functools.partial(_ffn_kernel, FF=FF, fc=fc),
        grid=(E, R // tm),
        in_specs=[pl.BlockSpec((None, tm, D), lambda e, i: (e, i, 0)),
                  pl.BlockSpec((None, D, FF), lambda e, i: (e, 0, 0)),
                  pl.BlockSpec((None, D, FF), lambda e, i: (e, 0, 0)),
                  pl.BlockSpec((None, FF, D), lambda e, i: (e, 0, 0)),
                  pl.BlockSpec((None, tm, LANES), lambda e, i: (e, i, 0))],
        out_specs=pl.BlockSpec((None, tm, D), lambda e, i: (e, i, 0)),
        out_shape=jax.ShapeDtypeStruct((E, R, D), BF16),
        scratch_shapes=[pltpu.VMEM((tm, D), F32)],
        compiler_params=_cparams(("parallel", "parallel")),
        name="ffn",
    )(xg, wg, wu, wd, gates)


def _combine_kernel(ss_ref, pos_ref, x1_ref, ye_ref, gfin_ref, out_ref, acc_ref, yall_ref, *, C, W, NT, final):
    E = ye_ref.shape[0]
    tm = x1_ref.shape[0]
    b = pl.program_id(0)
    i = pl.program_id(1)
    starts = []
    for e in range(E):
        row = (b * E + e) * (NT + 1) + i
        s0 = ss_ref[row]
        s1 = ss_ref[row + 1]
        a = pl.multiple_of(jnp.minimum((s0 // 16) * 16, C - W), 16)
        yall_ref[e * W:(e + 1) * W, :] = ye_ref[e, pl.ds(a, W), :]
        starts.append((a, s1))
    lane = lax.broadcasted_iota(I32, (tm, 2 * W), 1)
    first = lane < W
    pieces = []
    for e in range(0, E, 2):
        slot = jnp.where(first, starts[e][0] + lane, starts[e + 1][0] + lane - W)
        pcol = jnp.where(first, pos_ref[:, e:e + 1], pos_ref[:, e + 1:e + 2])
        pieces.append(jnp.where(pcol == slot, 1.0, 0.0).astype(BF16))
    onehot = jnp.concatenate(pieces, axis=1)
    acc_ref[...] = x1_ref[...] + jnp.dot(onehot, yall_ref[...], preferred_element_type=F32)
    lane_w = lax.broadcasted_iota(I32, (tm, W), 1)
    for e in range(E):
        a0, s1 = starts[e]
        n_extra = jnp.maximum((s1 - a0 - 1) // W, 0)
        pcol = pos_ref[:, e:e + 1]

        def chunk(k, carry, e=e, a0=a0, pcol=pcol):
            lo = a0 + k * W
            a = pl.multiple_of(jnp.minimum(lo, C - W), 16)
            slot = a + lane_w
            hit = jnp.logical_and(pcol == slot, slot >= lo)
            y = ye_ref[e, pl.ds(a, W), :]
            acc_ref[...] += jnp.dot(jnp.where(hit, 1.0, 0.0).astype(BF16), y, preferred_element_type=F32)
            return carry

        lax.fori_loop(1, n_extra + 1, chunk, 0)
    out = acc_ref[...]
    if final:
        out = _rms(out, gfin_ref[...])
    out_ref[...] = out


def _combine(ss, pos_t, x1, ye, gfin, C, final, tm=256, W=64):
    B, T, D = x1.shape
    E = ye.shape[0]
    tm = min(tm, T)
    NT = T // tm
    assert C % 16 == 0 and W % 16 == 0 and W <= C
    return pl.pallas_call(
        functools.partial(_combine_kernel, C=C, W=W, NT=NT, final=final),
        grid_spec=pltpu.PrefetchScalarGridSpec(
            num_scalar_prefetch=1,
            grid=(B, NT),
            in_specs=[pl.BlockSpec((None, tm, E), lambda b, i, ss: (b, i, 0)),
                      pl.BlockSpec((None, tm, D), lambda b, i, ss: (b, i, 0)),
                      pl.BlockSpec((E, None, C, D), lambda b, i, ss: (0, b, 0, 0), pipeline_mode=pl.Buffered(1)),
                      pl.BlockSpec((1, D), lambda b, i, ss: (0, 0))],
            out_specs=pl.BlockSpec((None, tm, D), lambda b, i, ss: (b, i, 0)),
            scratch_shapes=[pltpu.VMEM((tm, D), F32), pltpu.VMEM((E * W, D), BF16)],
        ),
        out_shape=jax.ShapeDtypeStruct((B, T, D), F32),
        compiler_params=_cparams(("arbitrary", "arbitrary")),
        name="combine",
    )(ss, pos_t, x1, ye, gfin.reshape(1, D))


def _rope_tables(T):
    pos = jnp.arange(T, dtype=F32)
    inv = ROPE_THETA ** (-jnp.arange(0, HEAD_DIM, 2, dtype=F32) / HEAD_DIM)
    ang = pos[:, None] * inv[None, :]
    cos = jnp.tile(jnp.cos(ang), (1, LANES // (HEAD_DIM // 2)))
    sin = jnp.sin(ang)
    sin = jnp.tile(jnp.concatenate([-sin, sin], axis=-1), (1, LANES // HEAD_DIM))
    return cos, sin


def _chunk_starts(pref, step, total):
    B, E, _ = pref.shape
    tail = jnp.full((B, E, 1), total, I32)
    return jnp.concatenate([pref[:, :, ::step], tail], axis=-1).reshape(-1)


def kernel(x, attn_norm, w_in, dil_out_norm, na_out_norm, na_rpb, w_out, ffn_norm, w_router,
           w_gate, w_up, w_down, final_norm):
    B, T, D = x.shape
    depth = w_in.shape[0]
    C = EC_CAPACITY * T // N_EXPERTS
    cos, sin = _rope_tables(T)
    tc_dispatch, tm_combine = LANES, min(256, T)
    for l in range(depth):
        qa, ka, va, qb, kb, vb = _qkv(x, attn_norm[l], w_in[l].astype(BF16), cos, sin)
        od = _dilated(qa, ka, va)
        on = _neighbourhood(qb, kb, vb, _na_bias_tables(na_rpb[l]))
        wr_t = w_router[l].T
        wr_hi = wr_t.astype(BF16)
        wr_lo = (wr_t - wr_hi.astype(F32)).astype(BF16)
        x1, h, aff_t = _post(od, on, x, w_out[l].astype(BF16), dil_out_norm[l], na_out_norm[l],
                             ffn_norm[l], wr_hi, wr_lo)
        pos, pref = _select(aff_t, C)
        xg, gates = _dispatch(_chunk_starts(pref, tc_dispatch, C), pos, aff_t, h, C, Tc=tc_dispatch)
        ye = _ffn(xg.reshape(N_EXPERTS, B * C, D), w_gate[l].astype(BF16), w_up[l].astype(BF16),
                  w_down[l].astype(BF16), gates.reshape(N_EXPERTS, B * C, LANES))
        x = _combine(_chunk_starts(pref, tm_combine, C), pos.transpose(0, 2, 1), x1,
                     ye.reshape(N_EXPERTS, B, C, D), final_norm, C, final=(l == depth - 1), tm=tm_combine)
    return x
```

```python
import functools

import jax
import jax.numpy as jnp
from jax import lax
from jax.experimental import pallas as pl
from jax.experimental.pallas import tpu as pltpu

F32 = jnp.float32
BF16 = jnp.bfloat16
I32 = jnp.int32

HEAD_DIM = 64
LANES = 128
PAIRS = 4
GROUP = PAIRS * LANES
ATTN_SCALE = HEAD_DIM ** -0.5
ROPE_THETA = 10000.0
DIL_PATTERNS = ((128, 1), (512, 4), (2048, 16))
GRID_W = 64
NA_ROWS = 8
NA_COLS = 16
N_EXPERTS = 16
EC_CAPACITY = 2
EPS = 1e-6
NEG_INF = -1e30

VMEM_LIMIT = 56 * 1024 * 1024

_NT = (((1,), (1,)), ((), ()))


def _cparams(sem, vmem=VMEM_LIMIT):
    return pltpu.CompilerParams(dimension_semantics=sem, vmem_limit_bytes=vmem)


def _head0_mask():
    return lax.broadcasted_iota(I32, (1, LANES), 1) < HEAD_DIM


def _clip(x, lo, hi):
    return min(max(x, lo), hi) if isinstance(x, int) else jnp.clip(x, lo, hi)


def _aligned(x, m):
    return x if isinstance(x, int) else pl.multiple_of(x, m)


def _qkv_kernel(x_ref, g_ref, w_ref, cos_ref, sin_ref,
                qa_ref, ka_ref, va_ref, qb_ref, kb_ref, vb_ref):
    x = x_ref[...]
    ms = jnp.mean(x * x, axis=-1, keepdims=True)
    h = (x * lax.rsqrt(ms + EPS) * g_ref[...]).astype(BF16)
    cos = cos_ref[...]
    sin = sin_ref[...]
    lane = lax.broadcasted_iota(I32, (1, LANES), 1)
    first_half = (lane % HEAD_DIM) < HEAD_DIM // 2
    outs = (qa_ref, ka_ref, va_ref, qb_ref, kb_ref, vb_ref)
    for gi, o_ref in enumerate(outs):
        y = jnp.dot(h, w_ref[:, gi * GROUP:(gi + 1) * GROUP], preferred_element_type=F32)
        for p in range(PAIRS):
            s = y[:, p * LANES:(p + 1) * LANES]
            if gi < 2:
                partner = jnp.where(first_half, pltpu.roll(s, LANES - HEAD_DIM // 2, 1),
                                    pltpu.roll(s, HEAD_DIM // 2, 1))
                s = s * cos + partner * sin
            if gi in (0, 3):
                s = s * ATTN_SCALE
            o_ref[p] = s.astype(o_ref.dtype)


def _qkv(x, g, w_bf16, cos, sin, tm=512):
    B, T, D = x.shape
    tm = min(tm, T)
    slab = lambda dt: jax.ShapeDtypeStruct((B, PAIRS, T, LANES), dt)
    out_spec = pl.BlockSpec((None, PAIRS, tm, LANES), lambda b, i: (b, 0, i, 0))
    return pl.pallas_call(
        _qkv_kernel,
        grid=(B, T // tm),
        in_specs=[
            pl.BlockSpec((None, tm, D), lambda b, i: (b, i, 0)),
            pl.BlockSpec((1, D), lambda b, i: (0, 0)),
            pl.BlockSpec((D, 6 * GROUP), lambda b, i: (0, 0)),
            pl.BlockSpec((tm, LANES), lambda b, i: (i, 0)),
            pl.BlockSpec((tm, LANES), lambda b, i: (i, 0)),
        ],
        out_specs=[out_spec] * 6,
        out_shape=[slab(F32), slab(F32), slab(F32), slab(BF16), slab(BF16), slab(BF16)],
        compiler_params=_cparams(("parallel", "parallel")),
        name="qkv",
    )(x, g.reshape(1, D), w_bf16, cos, sin)


def _dil_kernel(q_ref, k_ref, v_ref, o_ref, m_ref, l_ref, bias_ref, *, T, Lq, unroll):
    head0 = _head0_mask()
    n_br = len(DIL_PATTERNS)
    for bi, (window, d) in enumerate(DIL_PATTERNS):
        half = window // (2 * d)
        L = T // d
        Wn = Lq + 2 * half
        nb = L // Lq
        rel0 = (lax.broadcasted_iota(I32, (Lq, Wn), 1) - lax.broadcasted_iota(I32, (Lq, Wn), 0))
        for ci, off in enumerate((0, -half, -2 * half)):
            bias_ref[ci] = jnp.where(jnp.abs(rel0 + off) <= half, 0.0, NEG_INF)

        def block(idx, carry, bi=bi, d=d, half=half, L=L, Wn=Wn, nb=nb):
            r = idx // nb
            n = idx % nb
            m0 = n * Lq
            ws = jnp.clip(m0 - half, 0, L - Wn)

            def rows(start, size):
                if d == 1:
                    return pl.ds(start, size)
                return pl.ds(r + d * start, size, stride=d)

            q = q_ref[rows(m0, Lq), :]
            k = k_ref[rows(ws, Wn), :].astype(BF16)
            v = v_ref[rows(ws, Wn), :].astype(BF16)
            bias = bias_ref[jnp.where(n == 0, 0, jnp.where(n == nb - 1, 2, 1))]
            q2 = jnp.concatenate([jnp.where(head0, q, 0.0), jnp.where(head0, 0.0, q)], axis=0).astype(BF16)
            s = lax.dot_general(q2, k, _NT, preferred_element_type=F32)
            stats = []
            for hs in (s[:Lq], s[Lq:]):
                hs = hs + bias
                mh = jnp.max(hs, axis=-1, keepdims=True)
                p = jnp.exp(hs - mh)
                stats.append((mh, jnp.sum(p, axis=-1, keepdims=True), p.astype(BF16)))
            pv = jnp.dot(jnp.concatenate([stats[0][2], stats[1][2]], axis=0), v, preferred_element_type=F32)
            mb = jnp.where(head0, stats[0][0], stats[1][0])
            lb = jnp.where(head0, stats[0][1], stats[1][1])
            ab = jnp.where(head0, pv[:Lq], pv[Lq:])
            qrows = rows(m0, Lq)
            if bi == 0:
                m_ref[qrows, :] = mb
                l_ref[qrows, :] = lb
                o_ref[qrows, :] = ab
            else:
                mo = m_ref[qrows, :]
                mn = jnp.maximum(mo, mb)
                al = jnp.exp(mo - mn)
                be = jnp.exp(mb - mn)
                ln = al * l_ref[qrows, :] + be * lb
                an = al * o_ref[qrows, :] + be * ab
                if bi == n_br - 1:
                    o_ref[qrows, :] = an / ln
                else:
                    m_ref[qrows, :] = mn
                    l_ref[qrows, :] = ln
                    o_ref[qrows, :] = an
            return carry

        lax.fori_loop(0, d * nb, block, 0, unroll=unroll)


def _dilated(qa, ka, va, Lq=128, unroll=4):
    B, P, T, _ = qa.shape
    halves = {window // (2 * d) for window, d in DIL_PATTERNS}
    assert len(halves) == 1, "the band-mask scratch is sized for one half-width"
    half = halves.pop()
    for window, d in DIL_PATTERNS:
        L = T // d
        assert T % d == 0 and L % Lq == 0 and L // Lq >= 2 and Lq >= half and Lq + 2 * half <= L, (T, window, d)
    spec = pl.BlockSpec((None, None, T, LANES), lambda b, p: (b, p, 0, 0))
    return pl.pallas_call(
        functools.partial(_dil_kernel, T=T, Lq=Lq, unroll=unroll),
        grid=(B, P),
        in_specs=[spec, spec, spec],
        out_specs=spec,
        out_shape=jax.ShapeDtypeStruct((B, P, T, LANES), F32),
        scratch_shapes=[pltpu.VMEM((T, LANES), F32), pltpu.VMEM((T, LANES), F32),
                        pltpu.VMEM((3, Lq, Lq + 2 * half), F32)],
        compiler_params=_cparams(("parallel", "parallel")),
        name="dilated",
    )(qa, ka, va)


def _na_bias_tables(rpb):
    H = rpb.shape[0]
    qc = jnp.arange(GRID_W)
    kc = jnp.arange(GRID_W)
    kr = jnp.arange(NA_ROWS)
    shift = jnp.arange(NA_ROWS)
    cs = jnp.clip(qc - NA_COLS // 2, 0, GRID_W - NA_COLS)
    valid = (kc[None, :] >= cs[:, None]) & (kc[None, :] < cs[:, None] + NA_COLS)
    dcol = jnp.clip(kc[None, :] - qc[:, None] + (NA_COLS - 1), 0, 2 * NA_COLS - 2)
    drow = kr[None, :] - shift[:, None] + (NA_ROWS - 1)
    row_pick = (drow[:, :, None] == jnp.arange(2 * NA_ROWS - 1)).astype(F32)
    col_pick = (dcol[:, :, None] == jnp.arange(2 * NA_COLS - 1)).astype(F32)
    hp = lax.Precision.HIGHEST
    t1 = jnp.einsum('hij,ski->hskj', rpb.astype(F32), row_pick, precision=hp)
    tbl = jnp.einsum('hskj,qcj->hsqkc', t1, col_pick, precision=hp)
    tbl = jnp.where(valid[None, None, :, None, :], tbl, NEG_INF)
    tbl = tbl.reshape(H // 2, 2, NA_ROWS, GRID_W, NA_ROWS * GRID_W).transpose(0, 2, 1, 3, 4)
    return tbl.reshape(H // 2, NA_ROWS, 2 * GRID_W, NA_ROWS * GRID_W)


def _na_kernel(q_ref, k_ref, v_ref, tbl_ref, o_ref, s_buf, p_buf, *, rows, rpu):
    head0 = _head0_mask()
    n_units = rows // rpu

    def geom(r):
        rs = _clip(r - NA_ROWS // 2, 0, rows - NA_ROWS)
        qrows = pl.ds(_aligned(r * GRID_W, GRID_W), GRID_W)
        kwin = pl.ds(_aligned(rs * GRID_W, GRID_W), NA_ROWS * GRID_W)
        return r - rs, qrows, kwin

    def stage_a(u):
        for t in range(rpu):
            shift, qrows, kwin = geom(u * rpu + t)
            q = q_ref[qrows, :]
            zero = jnp.zeros_like(q)
            q2 = jnp.concatenate([jnp.where(head0, q, zero), jnp.where(head0, zero, q)], axis=0)
            s_buf[u % 2, t] = lax.dot_general(q2, k_ref[kwin, :], _NT, preferred_element_type=F32) + tbl_ref[shift]

    def stage_b(u):
        for t in range(rpu):
            s = s_buf[u % 2, t]
            p = jnp.exp(s - jnp.max(s, axis=-1, keepdims=True))
            p_buf[u % 2, t] = (p * (1.0 / jnp.sum(p, axis=-1, keepdims=True))).astype(BF16)

    def stage_c(u):
        for t in range(rpu):
            _, qrows, kwin = geom(u * rpu + t)
            o = jnp.dot(p_buf[u % 2, t], v_ref[kwin, :], preferred_element_type=F32)
            o_ref[qrows, :] = jnp.where(head0, o[:GRID_W], o[GRID_W:])

    stage_a(0)
    stage_a(1)
    stage_b(0)

    def body(u, carry):
        stage_c(u)
        stage_b(u + 1)
        stage_a(u + 2)
        return carry

    lax.fori_loop(0, n_units - 2, body, 0, unroll=2)
    stage_c(n_units - 2)
    stage_b(n_units - 1)
    stage_c(n_units - 1)


def _neighbourhood(qb, kb, vb, tbl, rpu=2):
    B, P, T, _ = qb.shape
    rows = T // GRID_W
    assert T % GRID_W == 0 and rows >= NA_ROWS and rows % rpu == 0 and rows // rpu >= 3
    spec = pl.BlockSpec((None, None, T, LANES), lambda b, p: (b, p, 0, 0))
    return pl.pallas_call(
        functools.partial(_na_kernel, rows=rows, rpu=rpu),
        grid=(B, P),
        in_specs=[spec, spec, spec,
                  pl.BlockSpec((None, NA_ROWS, 2 * GRID_W, NA_ROWS * GRID_W), lambda b, p: (p, 0, 0, 0))],
        out_specs=spec,
        out_shape=jax.ShapeDtypeStruct((B, P, T, LANES), F32),
        scratch_shapes=[pltpu.VMEM((2, rpu, 2 * GRID_W, NA_ROWS * GRID_W), F32),
                        pltpu.VMEM((2, rpu, 2 * GRID_W, NA_ROWS * GRID_W), BF16)],
        compiler_params=_cparams(("parallel", "parallel")),
        name="nbr",
    )(qb, kb, vb, tbl)


def _rms(a, g):
    ms = jnp.mean(a * a, axis=-1, keepdims=True)
    return a * lax.rsqrt(ms + EPS) * g


def _post_kernel(od_ref, on_ref, x_ref, wo_ref, gd_ref, gn_ref, gf_ref, wrh_ref, wrl_ref,
                 x1_ref, h_ref, aff_ref):
    tm = x_ref.shape[0]
    n_sub = 1
    for sub in range(n_sub):
        rows = slice(sub * (tm // n_sub), (sub + 1) * (tm // n_sub))
        dil = jnp.concatenate([od_ref[p, rows] for p in range(PAIRS)], axis=-1)
        nbr = jnp.concatenate([on_ref[p, rows] for p in range(PAIRS)], axis=-1)
        mixed = jnp.concatenate([_rms(dil, gd_ref[...]), _rms(nbr, gn_ref[...])], axis=-1).astype(BF16)
        x1 = x_ref[rows, :] + jnp.dot(mixed, wo_ref[...], preferred_element_type=F32)
        x1_ref[rows, :] = x1
        h = _rms(x1, gf_ref[...])
        hh = h.astype(BF16)
        hl = (h - hh.astype(F32)).astype(BF16)
        h_ref[rows, :] = hh
        wrh = wrh_ref[...]
        lt = (lax.dot_general(wrh, hh, _NT, preferred_element_type=F32)
              + lax.dot_general(wrh, hl, _NT, preferred_element_type=F32)
              + lax.dot_general(wrl_ref[...], hh, _NT, preferred_element_type=F32))
        m = jnp.max(lt, axis=0, keepdims=True)
        e = jnp.exp(lt - m)
        aff_ref[:, rows] = e / jnp.sum(e, axis=0, keepdims=True)


def _post(od, on, x, wo_bf16, gd, gn, gf, wr_hi, wr_lo, tm=512):
    B, T, D = x.shape
    tm = min(tm, T)
    E = wr_hi.shape[0]
    slab = pl.BlockSpec((None, PAIRS, tm, LANES), lambda b, i: (b, 0, i, 0))
    tok = pl.BlockSpec((None, tm, D), lambda b, i: (b, i, 0))
    full = lambda shape: pl.BlockSpec(shape, lambda b, i: (0,) * len(shape))
    return pl.pallas_call(
        _post_kernel,
        grid=(B, T // tm),
        in_specs=[slab, slab, tok, full((2 * GROUP, D)), full((1, GROUP)), full((1, GROUP)),
                  full((1, D)), full((E, D)), full((E, D))],
        out_specs=[tok, tok, pl.BlockSpec((None, E, tm), lambda b, i: (b, 0, i))],
        out_shape=[jax.ShapeDtypeStruct((B, T, D), F32), jax.ShapeDtypeStruct((B, T, D), BF16),
                   jax.ShapeDtypeStruct((B, E, T), F32)],
        compiler_params=_cparams(("parallel", "parallel")),
        name="post",
    )(od, on, x, wo_bf16, gd.reshape(1, GROUP), gn.reshape(1, GROUP), gf.reshape(1, D), wr_hi, wr_lo)


def _select_kernel(aff_ref, pos_ref, pref_ref, *, T, C):
    E = aff_ref.shape[0]
    bits = pltpu.bitcast(aff_ref[...], I32)
    t = jnp.zeros((E, 1), I32)
    for bit in range(30, -1, -1):
        cand = t | (1 << bit)
        cnt = jnp.sum(jnp.where(bits >= cand, 1.0, 0.0), axis=1, keepdims=True)
        t = jnp.where(cnt >= C, cand, t)
    gt = bits > t
    eq = bits == t
    need = C - jnp.sum(jnp.where(gt, 1.0, 0.0), axis=1, keepdims=True)
    tri = jnp.where(lax.broadcasted_iota(I32, (LANES, LANES), 0) < lax.broadcasted_iota(I32, (LANES, LANES), 1),
                    1.0, 0.0).astype(BF16)
    eq_carry = jnp.zeros((E, 1), F32)
    sel_carry = jnp.zeros((E, 1), F32)
    for c in range(T // LANES):
        sl = slice(c * LANES, (c + 1) * LANES)
        eqc = jnp.where(eq[:, sl], 1.0, 0.0)
        eq_rank = jnp.dot(eqc.astype(BF16), tri, preferred_element_type=F32) + eq_carry
        eq_carry = eq_carry + jnp.sum(eqc, axis=1, keepdims=True)
        sel = jnp.logical_or(gt[:, sl], jnp.logical_and(eq[:, sl], eq_rank < need))
        selc = jnp.where(sel, 1.0, 0.0)
        pref = (jnp.dot(selc.astype(BF16), tri, preferred_element_type=F32) + sel_carry).astype(I32)
        sel_carry = sel_carry + jnp.sum(selc, axis=1, keepdims=True)
        pref_ref[:, sl] = pref
        pos_ref[:, sl] = jnp.where(sel, pref, -1)


def _select(aff_t, C):
    B, E, T = aff_t.shape
    spec = pl.BlockSpec((None, E, T), lambda b: (b, 0, 0))
    return pl.pallas_call(
        functools.partial(_select_kernel, T=T, C=C),
        grid=(B,),
        in_specs=[spec],
        out_specs=[spec, spec],
        out_shape=[jax.ShapeDtypeStruct((B, E, T), I32), jax.ShapeDtypeStruct((B, E, T), I32)],
        compiler_params=_cparams(("parallel",)),
        name="select",
    )(aff_t)


def _count_below(ref, start, n, bound):
    lo = jnp.int32(0)
    step = 1 << (n.bit_length() - 1)
    while step:
        probe = lo + step
        ok = jnp.logical_and(probe <= n, ref[start + jnp.minimum(probe, n) - 1] < bound)
        lo = jnp.where(ok, probe, lo)
        step >>= 1
    return lo


def _dispatch_kernel(cc_ref, pos_ref, aff_ref, h_ref, xg_ref, gate_ref, acc_ref, gacc_ref, *, C, Cs, Tc, NC, KW):
    E = pl.num_programs(1)
    row = (pl.program_id(0) * E + pl.program_id(1)) * (NC + 1)
    for j in range(C // Cs):
        base = j * Cs

        c_lo = _count_below(cc_ref, row + 1, NC, base + 1)
        c_hi = _count_below(cc_ref, row, NC, base + Cs)
        slot = base + lax.broadcasted_iota(I32, (Cs, Tc), 0)
        c0 = jnp.minimum(c_lo, NC - KW)
        pw = pos_ref[pl.ds(c0, KW), :]
        aw = aff_ref[pl.ds(c0, KW), :]
        hits = [pw[kk:kk + 1, :] == slot for kk in range(KW)]
        onehot = jnp.concatenate([jnp.where(hit, 1.0, 0.0).astype(BF16) for hit in hits], axis=1)
        hwin = h_ref[pl.ds(pl.multiple_of(c0 * Tc, Tc), KW * Tc), :]
        acc_ref[...] = jnp.dot(onehot, hwin, preferred_element_type=F32)
        gsum = jnp.where(hits[0], aw[0:1, :], 0.0)
        for kk in range(1, KW):
            gsum = gsum + jnp.where(hits[kk], aw[kk:kk + 1, :], 0.0)
        gacc_ref[...] = jnp.sum(gsum, axis=1, keepdims=True)

        def chunk(c, carry, slot=slot):
            hit = pos_ref[pl.ds(c, 1), :] == slot
            hc = h_ref[pl.ds(pl.multiple_of(c * Tc, Tc), Tc), :]
            acc_ref[...] += jnp.dot(jnp.where(hit, 1.0, 0.0).astype(BF16), hc, preferred_element_type=F32)
            gacc_ref[...] += jnp.sum(jnp.where(hit, aff_ref[pl.ds(c, 1), :], 0.0), axis=1, keepdims=True)
            return carry

        lax.fori_loop(c0 + KW, c_hi, chunk, 0)
        xg_ref[base:base + Cs, :] = acc_ref[...].astype(BF16)
        gate_ref[base:base + Cs, :] = jnp.broadcast_to(gacc_ref[...], (Cs, LANES))


def _dispatch(cc, pos, aff_t, h, C, Cs=128, Tc=LANES, KW=12):
    B, E, T = pos.shape
    D = h.shape[-1]
    NC = T // Tc
    assert T % Tc == 0 and C % Cs == 0 and KW <= NC
    row_spec = pl.BlockSpec((None, None, NC, Tc), lambda b, e, cc: (b, e, 0, 0))
    return pl.pallas_call(
        functools.partial(_dispatch_kernel, C=C, Cs=Cs, Tc=Tc, NC=NC, KW=KW),
        grid_spec=pltpu.PrefetchScalarGridSpec(
            num_scalar_prefetch=1,
            grid=(B, E),
            in_specs=[row_spec, row_spec,
                      pl.BlockSpec((None, T, D), lambda b, e, cc: (b, 0, 0), pipeline_mode=pl.Buffered(1))],
            out_specs=[pl.BlockSpec((None, None, C, D), lambda b, e, cc: (e, b, 0, 0)),
                       pl.BlockSpec((None, None, C, LANES), lambda b, e, cc: (e, b, 0, 0))],
            scratch_shapes=[pltpu.VMEM((Cs, D), F32), pltpu.VMEM((Cs, 1), F32)],
        ),
        out_shape=[jax.ShapeDtypeStruct((E, B, C, D), BF16), jax.ShapeDtypeStruct((E, B, C, LANES), F32)],
        compiler_params=_cparams(("arbitrary", "arbitrary")),
        name="dispatch",
    )(cc, pos.reshape(B, E, NC, Tc), aff_t.reshape(B, E, NC, Tc), h)


def _ffn_kernel(x_ref, wg_ref, wu_ref, wd_ref, gate_ref, y_ref, acc_ref, *, FF, fc):
    x = x_ref[...]
    for kc in range(FF // fc):
        sl = slice(kc * fc, (kc + 1) * fc)
        g = jnp.dot(x, wg_ref[:, sl], preferred_element_type=F32)
        u = jnp.dot(x, wu_ref[:, sl], preferred_element_type=F32)
        a = (g * jax.nn.sigmoid(g) * u).astype(BF16)
        y = jnp.dot(a, wd_ref[sl, :], preferred_element_type=F32)
        if kc == 0:
            acc_ref[...] = y
        else:
            acc_ref[...] += y
    gate = gate_ref[...]
    for s in range(y_ref.shape[-1] // LANES):
        sl = slice(s * LANES, (s + 1) * LANES)
        y_ref[:, sl] = (acc_ref[:, sl] * gate).astype(y_ref.dtype)


def _ffn(xg, wg, wu, wd, gates, layer, tm=512, fc=256):
    E, R, D = xg.shape
    FF = wg.shape[-1]
    assert R % tm == 0 and FF % fc == 0
    return pl.pallas_call(
        functools.partial(_ffn_kernel, FF=FF, fc=fc),
        grid=(E, R // tm),
        in_specs=[pl.BlockSpec((None, tm, D), lambda e, i: (e, i, 0)),
                  pl.BlockSpec((None, None, D, FF), lambda e, i: (layer, e, 0, 0)),
                  pl.BlockSpec((None, None, D, FF), lambda e, i: (layer, e, 0, 0)),
                  pl.BlockSpec((None, None, FF, D), lambda e, i: (layer, e, 0, 0)),
                  pl.BlockSpec((None, tm, LANES), lambda e, i: (e, i, 0))],
        out_specs=pl.BlockSpec((None, tm, D), lambda e, i: (e, i, 0)),
        out_shape=jax.ShapeDtypeStruct((E, R, D), BF16),
        scratch_shapes=[pltpu.VMEM((tm, D), F32)],
        compiler_params=_cparams(("parallel", "parallel")),
        name="ffn",
    )(xg, wg, wu, wd, gates)


def _combine_kernel(ss_ref, pos_ref, x1_ref, ye_ref, gfin_ref, out_ref, acc_ref, yall_ref, *, C, W, NT, final):
    E = ye_ref.shape[0]
    tm = x1_ref.shape[0]
    b = pl.program_id(0)
    i = pl.program_id(1)
    starts = []
    for e in range(E):
        row = (b * E + e) * (NT + 1) + i
        s0 = ss_ref[row]
        s1 = ss_ref[row + 1]
        a = pl.multiple_of(jnp.minimum((s0 // 16) * 16, C - W), 16)
        yall_ref[e * W:(e + 1) * W, :] = ye_ref[e, pl.ds(a, W), :]
        starts.append((a, s1))
    lane = lax.broadcasted_iota(I32, (tm, 2 * W), 1)
    first = lane < W
    pieces = []
    for e in range(0, E, 2):
        slot = jnp.where(first, starts[e][0] + lane, starts[e + 1][0] + lane - W)
        pcol = jnp.where(first, pos_ref[:, e:e + 1], pos_ref[:, e + 1:e + 2])
        pieces.append(jnp.where(pcol == slot, 1.0, 0.0).astype(BF16))
    onehot = jnp.concatenate(pieces, axis=1)
    acc_ref[...] = x1_ref[...] + jnp.dot(onehot, yall_ref[...], preferred_element_type=F32)
    lane_w = lax.broadcasted_iota(I32, (tm, W), 1)
    for e in range(E):
        a0, s1 = starts[e]
        n_extra = jnp.maximum((s1 - a0 - 1) // W, 0)
        pcol = pos_ref[:, e:e + 1]

        def chunk(k, carry, e=e, a0=a0, pcol=pcol):
            lo = a0 + k * W
            a = pl.multiple_of(jnp.minimum(lo, C - W), 16)
            slot = a + lane_w
            hit = jnp.logical_and(pcol == slot, slot >= lo)
            y = ye_ref[e, pl.ds(a, W), :]
            acc_ref[...] += jnp.dot(jnp.where(hit, 1.0, 0.0).astype(BF16), y, preferred_element_type=F32)
            return carry

        lax.fori_loop(1, n_extra + 1, chunk, 0)
    out = acc_ref[...]
    if final:
        out = _rms(out, gfin_ref[...])
    out_ref[...] = out


def _combine(ss, pos_t, x1, ye, gfin, C, final, tm=256, W=64):
    B, T, D = x1.shape
    E = ye.shape[0]
    tm = min(tm, T)
    NT = T // tm
    assert C % 16 == 0 and W % 16 == 0 and W <= C
    return pl.pallas_call(
        functools.partial(_combine_kernel, C=C, W=W, NT=NT, final=final),
        grid_spec=pltpu.PrefetchScalarGridSpec(
            num_scalar_prefetch=1,
            grid=(B, NT),
            in_specs=[pl.BlockSpec((None, tm, E), lambda b, i, ss: (b, i, 0)),
                      pl.BlockSpec((None, tm, D), lambda b, i, ss: (b, i, 0)),
                      pl.BlockSpec((E, None, C, D), lambda b, i, ss: (0, b, 0, 0), pipeline_mode=pl.Buffered(1)),
                      pl.BlockSpec((1, D), lambda b, i, ss: (0, 0))],
            out_specs=pl.BlockSpec((None, tm, D), lambda b, i, ss: (b, i, 0)),
            scratch_shapes=[pltpu.VMEM((tm, D), F32), pltpu.VMEM((E * W, D), BF16)],
        ),
        out_shape=jax.ShapeDtypeStruct((B, T, D), F32),
        compiler_params=_cparams(("arbitrary", "arbitrary")),
        name="combine",
    )(ss, pos_t, x1, ye, gfin.reshape(1, D))


def _rope_tables(T):
    pos = jnp.arange(T, dtype=F32)
    inv = ROPE_THETA ** (-jnp.arange(0, HEAD_DIM, 2, dtype=F32) / HEAD_DIM)
    ang = pos[:, None] * inv[None, :]
    cos = jnp.tile(jnp.cos(ang), (1, LANES // (HEAD_DIM // 2)))
    sin = jnp.sin(ang)
    sin = jnp.tile(jnp.concatenate([-sin, sin], axis=-1), (1, LANES // HEAD_DIM))
    return cos, sin


def _chunk_starts(pref, step, total):
    B, E, _ = pref.shape
    tail = jnp.full((B, E, 1), total, I32)
    return jnp.concatenate([pref[:, :, ::step], tail], axis=-1).reshape(-1)


def kernel(x, attn_norm, w_in, dil_out_norm, na_out_norm, na_rpb, w_out, ffn_norm, w_router,
           w_gate, w_up, w_down, final_norm):
    B, T, D = x.shape
    depth = w_in.shape[0]
    C = EC_CAPACITY * T // N_EXPERTS
    cos, sin = _rope_tables(T)
    tc_dispatch, tm_combine = LANES, min(256, T)
    wg_bf, wu_bf, wd_bf = w_gate.astype(BF16), w_up.astype(BF16), w_down.astype(BF16)
    for l in range(depth):
        qa, ka, va, qb, kb, vb = _qkv(x, attn_norm[l], w_in[l].astype(BF16), cos, sin)
        od = _dilated(qa, ka, va)
        on = _neighbourhood(qb, kb, vb, _na_bias_tables(na_rpb[l]))
        wr_t = w_router[l].T
        wr_hi = wr_t.astype(BF16)
        wr_lo = (wr_t - wr_hi.astype(F32)).astype(BF16)
        x1, h, aff_t = _post(od, on, x, w_out[l].astype(BF16), dil_out_norm[l], na_out_norm[l],
                             ffn_norm[l], wr_hi, wr_lo)
        pos, pref = _select(aff_t, C)
        xg, gates = _dispatch(_chunk_starts(pref, tc_dispatch, C), pos, aff_t, h, C, Tc=tc_dispatch)
        ye = _ffn(xg.reshape(N_EXPERTS, B * C, D), wg_bf, wu_bf, wd_bf,
                  gates.reshape(N_EXPERTS, B * C, LANES), l)
        x = _combine(_chunk_starts(pref, tm_combine, C), pos.transpose(0, 2, 1), x1,
                     ye.reshape(N_EXPERTS, B, C, D), final_norm, C, final=(l == depth - 1), tm=tm_combine)
    return x
```

```python
import functools

import jax
import jax.numpy as jnp
from jax import lax
from jax.experimental import pallas as pl
from jax.experimental.pallas import tpu as pltpu

F32 = jnp.float32
BF16 = jnp.bfloat16
I32 = jnp.int32

HEAD_DIM = 64
LANES = 128
PAIRS = 4
GROUP = PAIRS * LANES
ATTN_SCALE = HEAD_DIM ** -0.5
ROPE_THETA = 10000.0
DIL_PATTERNS = ((128, 1), (512, 4), (2048, 16))
GRID_W = 64
NA_ROWS = 8
NA_COLS = 16
N_EXPERTS = 16
GROUP_E = 4
EC_CAPACITY = 2
EPS = 1e-6
NEG_INF = -1e30

VMEM_LIMIT = 56 * 1024 * 1024

_NT = (((1,), (1,)), ((), ()))


def _cparams(sem, vmem=VMEM_LIMIT):
    return pltpu.CompilerParams(dimension_semantics=sem, vmem_limit_bytes=vmem)


def _head0_mask():
    return lax.broadcasted_iota(I32, (1, LANES), 1) < HEAD_DIM


def _clip(x, lo, hi):
    return min(max(x, lo), hi) if isinstance(x, int) else jnp.clip(x, lo, hi)


def _aligned(x, m):
    return x if isinstance(x, int) else pl.multiple_of(x, m)


def _qkv_kernel(x_ref, g_ref, w_ref, cos_ref, sin_ref,
                qa_ref, ka_ref, va_ref, qb_ref, kb_ref, vb_ref):
    x = x_ref[...]
    ms = jnp.mean(x * x, axis=-1, keepdims=True)
    h = (x * lax.rsqrt(ms + EPS) * g_ref[...]).astype(BF16)
    cos = cos_ref[...]
    sin = sin_ref[...]
    lane = lax.broadcasted_iota(I32, (1, LANES), 1)
    first_half = (lane % HEAD_DIM) < HEAD_DIM // 2
    outs = (qa_ref, ka_ref, va_ref, qb_ref, kb_ref, vb_ref)
    for gi, o_ref in enumerate(outs):
        y = jnp.dot(h, w_ref[:, gi * GROUP:(gi + 1) * GROUP], preferred_element_type=F32)
        for p in range(PAIRS):
            s = y[:, p * LANES:(p + 1) * LANES]
            if gi < 2:
                partner = jnp.where(first_half, pltpu.roll(s, LANES - HEAD_DIM // 2, 1),
                                    pltpu.roll(s, HEAD_DIM // 2, 1))
                s = s * cos + partner * sin
            if gi in (0, 3):
                s = s * ATTN_SCALE
            o_ref[p] = s.astype(o_ref.dtype)


def _qkv(x, g, w_bf16, cos, sin, tm=512):
    B, T, D = x.shape
    tm = min(tm, T)
    slab = lambda dt: jax.ShapeDtypeStruct((B, PAIRS, T, LANES), dt)
    out_spec = pl.BlockSpec((None, PAIRS, tm, LANES), lambda b, i: (b, 0, i, 0))
    return pl.pallas_call(
        _qkv_kernel,
        grid=(B, T // tm),
        in_specs=[
            pl.BlockSpec((None, tm, D), lambda b, i: (b, i, 0)),
            pl.BlockSpec((1, D), lambda b, i: (0, 0)),
            pl.BlockSpec((D, 6 * GROUP), lambda b, i: (0, 0)),
            pl.BlockSpec((tm, LANES), lambda b, i: (i, 0)),
            pl.BlockSpec((tm, LANES), lambda b, i: (i, 0)),
        ],
        out_specs=[out_spec] * 6,
        out_shape=[slab(F32), slab(F32), slab(F32), slab(BF16), slab(BF16), slab(BF16)],
        compiler_params=_cparams(("parallel", "parallel")),
        name="qkv",
    )(x, g.reshape(1, D), w_bf16, cos, sin)


def _dil_kernel(q_ref, k_ref, v_ref, o_ref, m_ref, l_ref, bias_ref, *, T, Lq, unroll):
    head0 = _head0_mask()
    n_br = len(DIL_PATTERNS)
    for bi, (window, d) in enumerate(DIL_PATTERNS):
        half = window // (2 * d)
        L = T // d
        Wn = Lq + 2 * half
        nb = L // Lq
        rel0 = (lax.broadcasted_iota(I32, (Lq, Wn), 1) - lax.broadcasted_iota(I32, (Lq, Wn), 0))
        for ci, off in enumerate((0, -half, -2 * half)):
            bias_ref[ci] = jnp.where(jnp.abs(rel0 + off) <= half, 0.0, NEG_INF)

        def group(gi, carry, bi=bi, d=d, half=half, L=L, Wn=Wn, nb=nb):
            geo, q2s, ks, vs, biases = [], [], [], [], []
            for t in range(unroll):
                idx = gi * unroll + t
                r = idx // nb
                n = idx % nb
                m0 = n * Lq
                ws = jnp.clip(m0 - half, 0, L - Wn)

                def rows(start, size, r=r):
                    if d == 1:
                        return pl.ds(start, size)
                    return pl.ds(r + d * start, size, stride=d)

                q = q_ref[rows(m0, Lq), :]
                ks.append(k_ref[rows(ws, Wn), :].astype(BF16))
                vs.append(v_ref[rows(ws, Wn), :].astype(BF16))
                biases.append(bias_ref[jnp.where(n == 0, 0, jnp.where(n == nb - 1, 2, 1))])
                q2s.append(jnp.concatenate([jnp.where(head0, q, 0.0), jnp.where(head0, 0.0, q)], axis=0).astype(BF16))
                geo.append(rows(m0, Lq))
            s = jnp.einsum('bqd,bkd->bqk', jnp.stack(q2s), jnp.stack(ks), preferred_element_type=F32)
            ps, stats = [], []
            for t in range(unroll):
                st = []
                for hs in (s[t, :Lq], s[t, Lq:]):
                    hs = hs + biases[t]
                    mh = jnp.max(hs, axis=-1, keepdims=True)
                    p = jnp.exp(hs - mh)
                    st.append((mh, jnp.sum(p, axis=-1, keepdims=True), p.astype(BF16)))
                ps.append(jnp.concatenate([st[0][2], st[1][2]], axis=0))
                stats.append(st)
            pv = jnp.einsum('bqk,bkd->bqd', jnp.stack(ps), jnp.stack(vs), preferred_element_type=F32)
            for t in range(unroll):
                st = stats[t]
                mb = jnp.where(head0, st[0][0], st[1][0])
                lb = jnp.where(head0, st[0][1], st[1][1])
                ab = jnp.where(head0, pv[t, :Lq], pv[t, Lq:])
                qrows = geo[t]
                if bi == 0:
                    m_ref[qrows, :] = mb
                    l_ref[qrows, :] = lb
                    o_ref[qrows, :] = ab
                else:
                    mo = m_ref[qrows, :]
                    mn = jnp.maximum(mo, mb)
                    al = jnp.exp(mo - mn)
                    be = jnp.exp(mb - mn)
                    ln = al * l_ref[qrows, :] + be * lb
                    an = al * o_ref[qrows, :] + be * ab
                    if bi == n_br - 1:
                        o_ref[qrows, :] = an / ln
                    else:
                        m_ref[qrows, :] = mn
                        l_ref[qrows, :] = ln
                        o_ref[qrows, :] = an
            return carry

        lax.fori_loop(0, d * nb // unroll, group, 0)


def _dilated(qa, ka, va, Lq=128, unroll=4):
    B, P, T, _ = qa.shape
    halves = {window // (2 * d) for window, d in DIL_PATTERNS}
    assert len(halves) == 1, "the band-mask scratch is sized for one half-width"
    half = halves.pop()
    for window, d in DIL_PATTERNS:
        L = T // d
        assert T % d == 0 and L % Lq == 0 and L // Lq >= 2 and Lq >= half and Lq + 2 * half <= L, (T, window, d)
    spec = pl.BlockSpec((None, None, T, LANES), lambda b, p: (b, p, 0, 0))
    return pl.pallas_call(
        functools.partial(_dil_kernel, T=T, Lq=Lq, unroll=unroll),
        grid=(B, P),
        in_specs=[spec, spec, spec],
        out_specs=spec,
        out_shape=jax.ShapeDtypeStruct((B, P, T, LANES), F32),
        scratch_shapes=[pltpu.VMEM((T, LANES), F32), pltpu.VMEM((T, LANES), F32),
                        pltpu.VMEM((3, Lq, Lq + 2 * half), F32)],
        compiler_params=_cparams(("parallel", "parallel")),
        name="dilated",
    )(qa, ka, va)


def _na_bias_tables(rpb):
    H = rpb.shape[0]
    qc = jnp.arange(GRID_W)
    kc = jnp.arange(GRID_W)
    kr = jnp.arange(NA_ROWS)
    shift = jnp.arange(NA_ROWS)
    cs = jnp.clip(qc - NA_COLS // 2, 0, GRID_W - NA_COLS)
    valid = (kc[None, :] >= cs[:, None]) & (kc[None, :] < cs[:, None] + NA_COLS)
    dcol = jnp.clip(kc[None, :] - qc[:, None] + (NA_COLS - 1), 0, 2 * NA_COLS - 2)
    drow = kr[None, :] - shift[:, None] + (NA_ROWS - 1)
    row_pick = (drow[:, :, None] == jnp.arange(2 * NA_ROWS - 1)).astype(F32)
    col_pick = (dcol[:, :, None] == jnp.arange(2 * NA_COLS - 1)).astype(F32)
    hp = lax.Precision.HIGHEST
    t1 = jnp.einsum('hij,ski->hskj', rpb.astype(F32), row_pick, precision=hp)
    tbl = jnp.einsum('hskj,qcj->hsqkc', t1, col_pick, precision=hp)
    tbl = jnp.where(valid[None, None, :, None, :], tbl, NEG_INF)
    tbl = tbl.reshape(H // 2, 2, NA_ROWS, GRID_W, NA_ROWS * GRID_W).transpose(0, 2, 1, 3, 4)
    return tbl.reshape(H // 2, NA_ROWS, 2 * GRID_W, NA_ROWS * GRID_W)


def _na_kernel(q_ref, k_ref, v_ref, tbl_ref, o_ref, s_buf, p_buf, *, rows, rpu):
    head0 = _head0_mask()
    n_units = rows // rpu

    def geom(r):
        rs = _clip(r - NA_ROWS // 2, 0, rows - NA_ROWS)
        qrows = pl.ds(_aligned(r * GRID_W, GRID_W), GRID_W)
        kwin = pl.ds(_aligned(rs * GRID_W, GRID_W), NA_ROWS * GRID_W)
        return r - rs, qrows, kwin

    def stage_a(u):
        q2s, ks, biases = [], [], []
        for t in range(rpu):
            shift, qrows, kwin = geom(u * rpu + t)
            q = q_ref[qrows, :]
            zero = jnp.zeros_like(q)
            q2s.append(jnp.concatenate([jnp.where(head0, q, zero), jnp.where(head0, zero, q)], axis=0))
            ks.append(k_ref[kwin, :])
            biases.append(tbl_ref[shift])
        s = jnp.einsum('bqd,bkd->bqk', jnp.stack(q2s), jnp.stack(ks), preferred_element_type=F32)
        for t in range(rpu):
            s_buf[u % 2, t] = s[t] + biases[t]

    def stage_b(u):
        for t in range(rpu):
            s = s_buf[u % 2, t]
            p = jnp.exp(s - jnp.max(s, axis=-1, keepdims=True))
            p_buf[u % 2, t] = (p * (1.0 / jnp.sum(p, axis=-1, keepdims=True))).astype(BF16)

    def stage_c(u):
        geo = [geom(u * rpu + t) for t in range(rpu)]
        vs = jnp.stack([v_ref[kwin, :] for _, _, kwin in geo])
        o = jnp.einsum('bqk,bkd->bqd', p_buf[u % 2], vs, preferred_element_type=F32)
        for t in range(rpu):
            o_ref[geo[t][1], :] = jnp.where(head0, o[t, :GRID_W], o[t, GRID_W:])

    stage_a(0)
    stage_a(1)
    stage_b(0)

    def body(u, carry):
        stage_c(u)
        stage_b(u + 1)
        stage_a(u + 2)
        return carry

    lax.fori_loop(0, n_units - 2, body, 0, unroll=2)
    stage_c(n_units - 2)
    stage_b(n_units - 1)
    stage_c(n_units - 1)


def _neighbourhood(qb, kb, vb, tbl, rpu=2):
    B, P, T, _ = qb.shape
    rows = T // GRID_W
    assert T % GRID_W == 0 and rows >= NA_ROWS and rows % rpu == 0 and rows // rpu >= 3
    spec = pl.BlockSpec((None, None, T, LANES), lambda b, p: (b, p, 0, 0))
    return pl.pallas_call(
        functools.partial(_na_kernel, rows=rows, rpu=rpu),
        grid=(B, P),
        in_specs=[spec, spec, spec,
                  pl.BlockSpec((None, NA_ROWS, 2 * GRID_W, NA_ROWS * GRID_W), lambda b, p: (p, 0, 0, 0))],
        out_specs=spec,
        out_shape=jax.ShapeDtypeStruct((B, P, T, LANES), F32),
        scratch_shapes=[pltpu.VMEM((2, rpu, 2 * GRID_W, NA_ROWS * GRID_W), F32),
                        pltpu.VMEM((2, rpu, 2 * GRID_W, NA_ROWS * GRID_W), BF16)],
        compiler_params=_cparams(("parallel", "parallel")),
        name="nbr",
    )(qb, kb, vb, tbl)


def _rms(a, g):
    ms = jnp.mean(a * a, axis=-1, keepdims=True)
    return a * lax.rsqrt(ms + EPS) * g


def _post_kernel(od_ref, on_ref, x_ref, wo_ref, gd_ref, gn_ref, gf_ref, wrh_ref, wrl_ref,
                 x1_ref, h_ref, aff_ref):
    dil = jnp.concatenate([od_ref[p] for p in range(PAIRS)], axis=-1)
    nbr = jnp.concatenate([on_ref[p] for p in range(PAIRS)], axis=-1)
    mixed = jnp.concatenate([_rms(dil, gd_ref[...]), _rms(nbr, gn_ref[...])], axis=-1).astype(BF16)
    x1 = x_ref[...] + jnp.dot(mixed, wo_ref[...], preferred_element_type=F32)
    x1_ref[...] = x1
    h = _rms(x1, gf_ref[...])
    hh = h.astype(BF16)
    hl = (h - hh.astype(F32)).astype(BF16)
    h_ref[...] = hh
    wrh = wrh_ref[...]
    lt = (lax.dot_general(wrh, hh, _NT, preferred_element_type=F32)
          + lax.dot_general(wrh, hl, _NT, preferred_element_type=F32)
          + lax.dot_general(wrl_ref[...], hh, _NT, preferred_element_type=F32))
    m = jnp.max(lt, axis=0, keepdims=True)
    e = jnp.exp(lt - m)
    aff_ref[...] = e / jnp.sum(e, axis=0, keepdims=True)


def _post(od, on, x, wo_bf16, gd, gn, gf, wr_hi, wr_lo, tm=512):
    B, T, D = x.shape
    tm = min(tm, T)
    E = wr_hi.shape[0]
    slab = pl.BlockSpec((None, PAIRS, tm, LANES), lambda b, i: (b, 0, i, 0))
    tok = pl.BlockSpec((None, tm, D), lambda b, i: (b, i, 0))
    full = lambda shape: pl.BlockSpec(shape, lambda b, i: (0,) * len(shape))
    return pl.pallas_call(
        _post_kernel,
        grid=(B, T // tm),
        in_specs=[slab, slab, tok, full((2 * GROUP, D)), full((1, GROUP)), full((1, GROUP)),
                  full((1, D)), full((E, D)), full((E, D))],
        out_specs=[tok, tok, pl.BlockSpec((None, E, tm), lambda b, i: (b, 0, i))],
        out_shape=[jax.ShapeDtypeStruct((B, T, D), F32), jax.ShapeDtypeStruct((B, T, D), BF16),
                   jax.ShapeDtypeStruct((B, E, T), F32)],
        compiler_params=_cparams(("parallel", "parallel")),
        name="post",
    )(od, on, x, wo_bf16, gd.reshape(1, GROUP), gn.reshape(1, GROUP), gf.reshape(1, D), wr_hi, wr_lo)


def _select_kernel(aff_ref, pos_ref, pref_ref, *, T, C):
    E = aff_ref.shape[0]
    bits = pltpu.bitcast(aff_ref[...], I32)
    t = jnp.zeros((E, 1), I32)
    for bit in range(30, -1, -1):
        cand = t | (1 << bit)
        cnt = jnp.sum(jnp.where(bits >= cand, 1.0, 0.0), axis=1, keepdims=True)
        t = jnp.where(cnt >= C, cand, t)
    gt = bits > t
    eq = bits == t
    need = C - jnp.sum(jnp.where(gt, 1.0, 0.0), axis=1, keepdims=True)
    tri = jnp.where(lax.broadcasted_iota(I32, (LANES, LANES), 0) < lax.broadcasted_iota(I32, (LANES, LANES), 1),
                    1.0, 0.0).astype(BF16)
    eq_carry = jnp.zeros((E, 1), F32)
    sel_carry = jnp.zeros((E, 1), F32)
    for c in range(T // LANES):
        sl = slice(c * LANES, (c + 1) * LANES)
        eqc = jnp.where(eq[:, sl], 1.0, 0.0)
        eq_rank = jnp.dot(eqc.astype(BF16), tri, preferred_element_type=F32) + eq_carry
        eq_carry = eq_carry + jnp.sum(eqc, axis=1, keepdims=True)
        sel = jnp.logical_or(gt[:, sl], jnp.logical_and(eq[:, sl], eq_rank < need))
        selc = jnp.where(sel, 1.0, 0.0)
        pref = (jnp.dot(selc.astype(BF16), tri, preferred_element_type=F32) + sel_carry).astype(I32)
        sel_carry = sel_carry + jnp.sum(selc, axis=1, keepdims=True)
        pref_ref[:, sl] = pref
        pos_ref[:, sl] = jnp.where(sel, pref, -1)


def _select(aff_t, C):
    B, E, T = aff_t.shape
    spec = pl.BlockSpec((None, E, T), lambda b: (b, 0, 0))
    return pl.pallas_call(
        functools.partial(_select_kernel, T=T, C=C),
        grid=(B,),
        in_specs=[spec],
        out_specs=[spec, spec],
        out_shape=[jax.ShapeDtypeStruct((B, E, T), I32), jax.ShapeDtypeStruct((B, E, T), I32)],
        compiler_params=_cparams(("parallel",)),
        name="select",
    )(aff_t)


def _count_below(ref, start, n, bound):
    lo = jnp.int32(0)
    step = 1 << (n.bit_length() - 1)
    while step:
        probe = lo + step
        ok = jnp.logical_and(probe <= n, ref[start + jnp.minimum(probe, n) - 1] < bound)
        lo = jnp.where(ok, probe, lo)
        step >>= 1
    return lo


def _dispatch_kernel(cc_ref, pos_ref, aff_ref, h_ref, xg_ref, gate_ref, acc_ref, gacc_ref, *, C, Cs, Tc, NC, KW):
    E = pl.num_programs(1)
    row = (pl.program_id(0) * E + pl.program_id(1)) * (NC + 1)
    for j in range(C // Cs):
        base = j * Cs

        c_lo = _count_below(cc_ref, row + 1, NC, base + 1)
        c_hi = _count_below(cc_ref, row, NC, base + Cs)
        slot = base + lax.broadcasted_iota(I32, (Cs, Tc), 0)
        c0 = jnp.minimum(c_lo, NC - KW)
        pw = pos_ref[pl.ds(c0, KW), :]
        aw = aff_ref[pl.ds(c0, KW), :]
        hits = [pw[kk:kk + 1, :] == slot for kk in range(KW)]
        onehot = jnp.concatenate([jnp.where(hit, 1.0, 0.0).astype(BF16) for hit in hits], axis=1)
        hwin = h_ref[pl.ds(pl.multiple_of(c0 * Tc, Tc), KW * Tc), :]
        acc_ref[...] = jnp.dot(onehot, hwin, preferred_element_type=F32)
        gsum = jnp.where(hits[0], aw[0:1, :], 0.0)
        for kk in range(1, KW):
            gsum = gsum + jnp.where(hits[kk], aw[kk:kk + 1, :], 0.0)
        gacc_ref[...] = jnp.sum(gsum, axis=1, keepdims=True)

        def chunk(c, carry, slot=slot):
            hit = pos_ref[pl.ds(c, 1), :] == slot
            hc = h_ref[pl.ds(pl.multiple_of(c * Tc, Tc), Tc), :]
            acc_ref[...] += jnp.dot(jnp.where(hit, 1.0, 0.0).astype(BF16), hc, preferred_element_type=F32)
            gacc_ref[...] += jnp.sum(jnp.where(hit, aff_ref[pl.ds(c, 1), :], 0.0), axis=1, keepdims=True)
            return carry

        lax.fori_loop(c0 + KW, c_hi, chunk, 0)
        xg_ref[base:base + Cs, :] = acc_ref[...].astype(BF16)
        gate_ref[base:base + Cs, :] = jnp.broadcast_to(gacc_ref[...], (Cs, LANES))


def _dispatch(cc, pos, aff_t, h, C, Cs=128, Tc=LANES, KW=12):
    B, E, T = pos.shape
    D = h.shape[-1]
    NC = T // Tc
    assert T % Tc == 0 and C % Cs == 0 and KW <= NC
    row_spec = pl.BlockSpec((None, None, NC, Tc), lambda b, e, cc: (b, e, 0, 0))
    return pl.pallas_call(
        functools.partial(_dispatch_kernel, C=C, Cs=Cs, Tc=Tc, NC=NC, KW=KW),
        grid_spec=pltpu.PrefetchScalarGridSpec(
            num_scalar_prefetch=1,
            grid=(B, E),
            in_specs=[row_spec, row_spec,
                      pl.BlockSpec((None, T, D), lambda b, e, cc: (b, 0, 0), pipeline_mode=pl.Buffered(1))],
            out_specs=[pl.BlockSpec((None, None, C, D), lambda b, e, cc: (e, b, 0, 0)),
                       pl.BlockSpec((None, None, C, LANES), lambda b, e, cc: (e, b, 0, 0))],
            scratch_shapes=[pltpu.VMEM((Cs, D), F32), pltpu.VMEM((Cs, 1), F32)],
        ),
        out_shape=[jax.ShapeDtypeStruct((E, B, C, D), BF16), jax.ShapeDtypeStruct((E, B, C, LANES), F32)],
        compiler_params=_cparams(("arbitrary", "arbitrary")),
        name="dispatch",
    )(cc, pos.reshape(B, E, NC, Tc), aff_t.reshape(B, E, NC, Tc), h)


def _ffn_kernel(x_ref, wg_ref, wu_ref, wd_ref, gate_ref, y_ref, acc_ref, *, FF, fc):
    x = x_ref[...]
    for kc in range(FF // fc):
        sl = slice(kc * fc, (kc + 1) * fc)
        g = jnp.dot(x, wg_ref[:, sl], preferred_element_type=F32)
        u = jnp.dot(x, wu_ref[:, sl], preferred_element_type=F32)
        a = (g * jax.nn.sigmoid(g) * u).astype(BF16)
        y = jnp.dot(a, wd_ref[sl, :], preferred_element_type=F32)
        if kc == 0:
            acc_ref[...] = y
        else:
            acc_ref[...] += y
    gate = gate_ref[...]
    for s in range(y_ref.shape[-1] // LANES):
        sl = slice(s * LANES, (s + 1) * LANES)
        y_ref[:, sl] = (acc_ref[:, sl] * gate).astype(y_ref.dtype)


def _ffn(xg, wg, wu, wd, gates, layer, tm=1024, fc=256):
    E, R, D = xg.shape
    FF = wg.shape[-1]
    tm = min(tm, R)
    assert R % tm == 0 and FF % fc == 0
    return pl.pallas_call(
        functools.partial(_ffn_kernel, FF=FF, fc=fc),
        grid=(E, R // tm),
        in_specs=[pl.BlockSpec((None, tm, D), lambda e, i: (e, i, 0)),
                  pl.BlockSpec((None, None, D, FF), lambda e, i: (layer, e, 0, 0)),
                  pl.BlockSpec((None, None, D, FF), lambda e, i: (layer, e, 0, 0)),
                  pl.BlockSpec((None, None, FF, D), lambda e, i: (layer, e, 0, 0)),
                  pl.BlockSpec((None, tm, LANES), lambda e, i: (e, i, 0))],
        out_specs=pl.BlockSpec((None, tm, D), lambda e, i: (e, i, 0)),
        out_shape=jax.ShapeDtypeStruct((E, R, D), BF16),
        scratch_shapes=[pltpu.VMEM((tm, D), F32)],
        compiler_params=_cparams(("parallel", "parallel")),
        name="ffn",
    )(xg, wg, wu, wd, gates)


def _combine_kernel(ss_ref, pos_ref, x1_ref, ye_ref, gfin_ref, out_ref, acc_ref, yall_ref, *, C, W, NT, final):
    E = ye_ref.shape[0]
    tm = x1_ref.shape[0]
    b = pl.program_id(0)
    i = pl.program_id(1)
    starts = []
    for e in range(E):
        row = (b * E + e) * (NT + 1) + i
        s0 = ss_ref[row]
        s1 = ss_ref[row + 1]
        a = pl.multiple_of(jnp.minimum((s0 // 16) * 16, C - W), 16)
        starts.append((a, jnp.maximum((s1 - a - 1) // W, 0)))
    lane = lax.broadcasted_iota(I32, (tm, 2 * W), 1)
    first = lane < W
    acc = x1_ref[...]
    for g in range(0, E, GROUP_E):
        pieces = []
        for e in range(g, g + GROUP_E, 2):
            for ee in (e, e + 1):
                yall_ref[ee * W:(ee + 1) * W, :] = ye_ref[ee, pl.ds(starts[ee][0], W), :]
            slot = jnp.where(first, starts[e][0] + lane, starts[e + 1][0] + lane - W)
            pcol = jnp.where(first, pos_ref[:, e:e + 1], pos_ref[:, e + 1:e + 2])
            pieces.append(jnp.where(pcol == slot, 1.0, 0.0).astype(BF16))
        onehot = jnp.concatenate(pieces, axis=1)
        acc = acc + jnp.dot(onehot, yall_ref[g * W:(g + GROUP_E) * W, :], preferred_element_type=F32)
    acc_ref[...] = acc
    n_extra_all = starts[0][1]
    for e in range(1, E):
        n_extra_all = n_extra_all + starts[e][1]

    @pl.when(n_extra_all > 0)
    def _():
        lane_w = lax.broadcasted_iota(I32, (tm, W), 1)
        for e in range(E):
            a0, n_extra = starts[e]
            pcol = pos_ref[:, e:e + 1]

            def chunk(k, carry, e=e, a0=a0, pcol=pcol):
                lo = a0 + k * W
                a = pl.multiple_of(jnp.minimum(lo, C - W), 16)
                slot = a + lane_w
                hit = jnp.logical_and(pcol == slot, slot >= lo)
                y = ye_ref[e, pl.ds(a, W), :]
                acc_ref[...] += jnp.dot(jnp.where(hit, 1.0, 0.0).astype(BF16), y, preferred_element_type=F32)
                return carry

            lax.fori_loop(1, n_extra + 1, chunk, 0)

    out = acc_ref[...]
    if final:
        out = _rms(out, gfin_ref[...])
    out_ref[...] = out


def _combine(ss, pos_t, x1, ye, gfin, C, final, tm=256, W=64):
    B, T, D = x1.shape
    E = ye.shape[0]
    tm = min(tm, T)
    NT = T // tm
    assert C % 16 == 0 and W % 16 == 0 and W <= C
    return pl.pallas_call(
        functools.partial(_combine_kernel, C=C, W=W, NT=NT, final=final),
        grid_spec=pltpu.PrefetchScalarGridSpec(
            num_scalar_prefetch=1,
            grid=(B, NT),
            in_specs=[pl.BlockSpec((None, tm, E), lambda b, i, ss: (b, i, 0)),
                      pl.BlockSpec((None, tm, D), lambda b, i, ss: (b, i, 0)),
                      pl.BlockSpec((E, None, C, D), lambda b, i, ss: (0, b, 0, 0), pipeline_mode=pl.Buffered(1)),
                      pl.BlockSpec((1, D), lambda b, i, ss: (0, 0))],
            out_specs=pl.BlockSpec((None, tm, D), lambda b, i, ss: (b, i, 0)),
            scratch_shapes=[pltpu.VMEM((tm, D), F32), pltpu.VMEM((E * W, D), BF16)],
        ),
        out_shape=jax.ShapeDtypeStruct((B, T, D), F32),
        compiler_params=_cparams(("arbitrary", "arbitrary")),
        name="combine",
    )(ss, pos_t, x1, ye, gfin.reshape(1, D))


def _rope_tables(T):
    pos = jnp.arange(T, dtype=F32)
    inv = ROPE_THETA ** (-jnp.arange(0, HEAD_DIM, 2, dtype=F32) / HEAD_DIM)
    ang = pos[:, None] * inv[None, :]
    cos = jnp.tile(jnp.cos(ang), (1, LANES // (HEAD_DIM // 2)))
    sin = jnp.sin(ang)
    sin = jnp.tile(jnp.concatenate([-sin, sin], axis=-1), (1, LANES // HEAD_DIM))
    return cos, sin


def _chunk_starts(pref, step, total):
    B, E, _ = pref.shape
    tail = jnp.full((B, E, 1), total, I32)
    return jnp.concatenate([pref[:, :, ::step], tail], axis=-1).reshape(-1)


def kernel(x, attn_norm, w_in, dil_out_norm, na_out_norm, na_rpb, w_out, ffn_norm, w_router,
           w_gate, w_up, w_down, final_norm):
    B, T, D = x.shape
    depth = w_in.shape[0]
    C = EC_CAPACITY * T // N_EXPERTS
    cos, sin = _rope_tables(T)
    tc_dispatch, tm_combine = LANES, min(256, T)
    wg_bf, wu_bf, wd_bf = w_gate.astype(BF16), w_up.astype(BF16), w_down.astype(BF16)
    for l in range(depth):
        qa, ka, va, qb, kb, vb = _qkv(x, attn_norm[l], w_in[l].astype(BF16), cos, sin)
        od = _dilated(qa, ka, va)
        on = _neighbourhood(qb, kb, vb, _na_bias_tables(na_rpb[l]))
        wr_t = w_router[l].T
        wr_hi = wr_t.astype(BF16)
        wr_lo = (wr_t - wr_hi.astype(F32)).astype(BF16)
        x1, h, aff_t = _post(od, on, x, w_out[l].astype(BF16), dil_out_norm[l], na_out_norm[l],
                             ffn_norm[l], wr_hi, wr_lo)
        pos, pref = _select(aff_t, C)
        xg, gates = _dispatch(_chunk_starts(pref, tc_dispatch, C), pos, aff_t, h, C, Tc=tc_dispatch)
        ye = _ffn(xg.reshape(N_EXPERTS, B * C, D), wg_bf, wu_bf, wd_bf,
                  gates.reshape(N_EXPERTS, B * C, LANES), l)
        x = _combine(_chunk_starts(pref, tm_combine, C), pos.transpose(0, 2, 1), x1,
                     ye.reshape(N_EXPERTS, B, C, D), final_norm, C, final=(l == depth - 1), tm=tm_combine)
    return x
```

```python
import functools

import jax
import jax.numpy as jnp
from jax import lax
from jax.experimental import pallas as pl
from jax.experimental.pallas import tpu as pltpu

F32 = jnp.float32
BF16 = jnp.bfloat16
I32 = jnp.int32

HEAD_DIM = 64
LANES = 128
PAIRS = 4
GROUP = PAIRS * LANES
ATTN_SCALE = HEAD_DIM ** -0.5
ROPE_THETA = 10000.0
DIL_PATTERNS = ((128, 1), (512, 4), (2048, 16))
GRID_W = 64
NA_ROWS = 8
NA_COLS = 16
N_EXPERTS = 16
GROUP_E = 4
EC_CAPACITY = 2
EPS = 1e-6
NEG_INF = -1e30

VMEM_LIMIT = 56 * 1024 * 1024

_NT = (((1,), (1,)), ((), ()))


def _cparams(sem, vmem=VMEM_LIMIT):
    return pltpu.CompilerParams(dimension_semantics=sem, vmem_limit_bytes=vmem)


def _head0_mask():
    return lax.broadcasted_iota(I32, (1, LANES), 1) < HEAD_DIM


def _clip(x, lo, hi):
    return min(max(x, lo), hi) if isinstance(x, int) else jnp.clip(x, lo, hi)


def _aligned(x, m):
    return x if isinstance(x, int) else pl.multiple_of(x, m)


def _qkv_kernel(x_ref, g_ref, w_ref, cos_ref, sin_ref, *refs):
    n_dil = len(DIL_PATTERNS)
    dil_refs = [refs[t * n_dil:(t + 1) * n_dil] for t in range(3)]
    nbr_refs = refs[3 * n_dil:3 * n_dil + 3]
    ys_ref = refs[-1]
    tm = x_ref.shape[0]
    x = x_ref[...]
    ms = jnp.mean(x * x, axis=-1, keepdims=True)
    h = (x * lax.rsqrt(ms + EPS) * g_ref[...]).astype(BF16)
    cos = cos_ref[...]
    sin = sin_ref[...]
    lane = lax.broadcasted_iota(I32, (1, LANES), 1)
    first_half = (lane % HEAD_DIM) < HEAD_DIM // 2
    for gi in range(6):
        y = jnp.dot(h, w_ref[:, gi * GROUP:(gi + 1) * GROUP], preferred_element_type=F32)
        for p in range(PAIRS):
            s = y[:, p * LANES:(p + 1) * LANES]
            if gi < 2:
                partner = jnp.where(first_half, pltpu.roll(s, LANES - HEAD_DIM // 2, 1),
                                    pltpu.roll(s, HEAD_DIM // 2, 1))
                s = s * cos + partner * sin
            if gi in (0, 3):
                s = s * ATTN_SCALE
            if gi >= 3:
                nbr_refs[gi - 3][p] = s.astype(BF16)
                continue
            ys_ref[gi * PAIRS + p] = s
            for (_, d), o_ref in zip(DIL_PATTERNS, dil_refs[gi]):
                if d == 1:
                    o_ref[p, 0] = s.astype(BF16)
    for t in range(3):
        for (_, d), o_ref in zip(DIL_PATTERNS, dil_refs[t]):
            if d == 1:
                continue

            def regroup(p, carry, t=t, d=d, o_ref=o_ref):
                for r in range(d):
                    o_ref[p, r] = ys_ref[t * PAIRS + p, pl.ds(r, tm // d, stride=d), :].astype(BF16)
                return carry

            lax.fori_loop(0, PAIRS, regroup, 0)


def _qkv(x, g, w_bf16, cos, sin, tm=512):
    B, T, D = x.shape
    tm = min(tm, T)
    dils = [d for _, d in DIL_PATTERNS]
    assert all(tm % (16 * d) == 0 for d in dils), "bf16 rows travel in packs of 16"
    dil_shapes = [jax.ShapeDtypeStruct((B, PAIRS, d, T // d, LANES), BF16) for d in dils]
    dil_specs = [pl.BlockSpec((None, PAIRS, d, tm // d, LANES), lambda b, i: (b, 0, 0, i, 0)) for d in dils]
    nbr_shape = jax.ShapeDtypeStruct((B, PAIRS, T, LANES), BF16)
    nbr_spec = pl.BlockSpec((None, PAIRS, tm, LANES), lambda b, i: (b, 0, i, 0))
    outs = pl.pallas_call(
        _qkv_kernel,
        grid=(B, T // tm),
        in_specs=[
            pl.BlockSpec((None, tm, D), lambda b, i: (b, i, 0)),
            pl.BlockSpec((1, D), lambda b, i: (0, 0)),
            pl.BlockSpec((D, 6 * GROUP), lambda b, i: (0, 0)),
            pl.BlockSpec((tm, LANES), lambda b, i: (i, 0)),
            pl.BlockSpec((tm, LANES), lambda b, i: (i, 0)),
        ],
        out_specs=dil_specs * 3 + [nbr_spec] * 3,
        out_shape=dil_shapes * 3 + [nbr_shape] * 3,
        scratch_shapes=[pltpu.VMEM((3 * PAIRS, tm, LANES), F32)],
        compiler_params=_cparams(("parallel", "parallel")),
        name="qkv",
    )(x, g.reshape(1, D), w_bf16, cos, sin)
    n = len(dils)
    return (tuple(outs[:n]), tuple(outs[n:2 * n]), tuple(outs[2 * n:3 * n])) + tuple(outs[3 * n:])


def _dil_kernel(*refs, T, Lq, unroll):
    n_br = len(DIL_PATTERNS)
    q_refs, k_refs, v_refs = refs[:n_br], refs[n_br:2 * n_br], refs[2 * n_br:3 * n_br]
    o_ref, lse_ref, bias_ref = refs[3 * n_br:]
    head0 = _head0_mask()
    for bi, (window, d) in enumerate(DIL_PATTERNS):
        q_ref, k_ref, v_ref = q_refs[bi], k_refs[bi], v_refs[bi]
        half = window // (2 * d)
        L = T // d
        Wn = Lq + 2 * half
        nb = L // Lq
        rel0 = (lax.broadcasted_iota(I32, (Lq, Wn), 1) - lax.broadcasted_iota(I32, (Lq, Wn), 0))
        for ci, off in enumerate((0, -half, -2 * half)):
            bias_ref[ci] = jnp.where(jnp.abs(rel0 + off) <= half, 0.0, NEG_INF)

        def group(gi, carry, bi=bi, d=d, half=half, L=L, Wn=Wn, nb=nb):
            geo, q2s, ks, vs, biases = [], [], [], [], []
            for t in range(unroll):
                idx = gi * unroll + t
                r = idx // nb
                n = idx % nb
                m0 = n * Lq
                ws = jnp.clip(m0 - half, 0, L - Wn)

                q = q_ref[r, pl.ds(pl.multiple_of(m0, Lq), Lq), :]
                kvrows = pl.ds(pl.multiple_of(ws, half), Wn)
                ks.append(k_ref[r, kvrows, :])
                vs.append(v_ref[r, kvrows, :])
                biases.append(bias_ref[jnp.where(n == 0, 0, jnp.where(n == nb - 1, 2, 1))])
                zero = jnp.zeros_like(q)
                q2s.append(jnp.concatenate([jnp.where(head0, q, zero), jnp.where(head0, zero, q)], axis=0))
                geo.append(pl.ds(m0, Lq) if d == 1 else pl.ds(r + d * m0, Lq, stride=d))
            s = jnp.einsum('bqd,bkd->bqk', jnp.stack(q2s), jnp.stack(ks), preferred_element_type=F32)
            ps, stats = [], []
            for t in range(unroll):
                st = []
                for hs in (s[t, :Lq], s[t, Lq:]):
                    hs = hs + biases[t]
                    mh = jnp.max(hs, axis=-1, keepdims=True)
                    p = jnp.exp(hs - mh)
                    st.append((mh, jnp.sum(p, axis=-1, keepdims=True), p.astype(BF16)))
                ps.append(jnp.concatenate([st[0][2], st[1][2]], axis=0))
                stats.append(st)
            pv = jnp.einsum('bqk,bkd->bqd', jnp.stack(ps), jnp.stack(vs), preferred_element_type=F32)
            for t in range(unroll):
                st = stats[t]
                lb = jnp.where(head0, st[0][1], st[1][1])
                lse_b = jnp.where(head0, st[0][0], st[1][0]) + jnp.log(lb)
                ob = jnp.where(head0, pv[t, :Lq], pv[t, Lq:]) / lb
                qrows = geo[t]
                if bi == 0:
                    lse_ref[qrows, :] = lse_b
                    o_ref[qrows, :] = ob
                else:
                    lse_o = lse_ref[qrows, :]
                    mn = jnp.maximum(lse_o, lse_b)
                    wo = jnp.exp(lse_o - mn)
                    wb = jnp.exp(lse_b - mn)
                    den = wo + wb
                    o_ref[qrows, :] = (wo * o_ref[qrows, :] + wb * ob) / den
                    if bi < n_br - 1:
                        lse_ref[qrows, :] = mn + jnp.log(den)
            return carry

        lax.fori_loop(0, d * nb // unroll, group, 0)


def _dilated(qs, ks, vs, Lq=128, unroll=4):
    B, P = qs[0].shape[:2]
    T = qs[0].shape[2] * qs[0].shape[3]
    halves = {window // (2 * d) for window, d in DIL_PATTERNS}
    assert len(halves) == 1, "the band-mask scratch is sized for one half-width"
    half = halves.pop()
    assert half % 16 == 0, "bf16 rows travel in packs of 16"
    for window, d in DIL_PATTERNS:
        L = T // d
        assert T % d == 0 and L % Lq == 0 and L // Lq >= 2 and Lq >= half and Lq + 2 * half <= L, (T, window, d)
        assert (d * (L // Lq)) % unroll == 0
    in_specs = [pl.BlockSpec((None, None, d, T // d, LANES), lambda b, p: (b, p, 0, 0, 0))
                for _ in range(3) for _, d in DIL_PATTERNS]
    out_spec = pl.BlockSpec((None, None, T, LANES), lambda b, p: (b, p, 0, 0))
    return pl.pallas_call(
        functools.partial(_dil_kernel, T=T, Lq=Lq, unroll=unroll),
        grid=(B, P),
        in_specs=in_specs,
        out_specs=out_spec,
        out_shape=jax.ShapeDtypeStruct((B, P, T, LANES), F32),
        scratch_shapes=[pltpu.VMEM((T, LANES), F32), pltpu.VMEM((3, Lq, Lq + 2 * half), F32)],
        compiler_params=_cparams(("parallel", "parallel")),
        name="dilated",
    )(*qs, *ks, *vs)


def _na_bias_tables(rpb):
    H = rpb.shape[0]
    qc = jnp.arange(GRID_W)
    kc = jnp.arange(GRID_W)
    kr = jnp.arange(NA_ROWS)
    shift = jnp.arange(NA_ROWS)
    cs = jnp.clip(qc - NA_COLS // 2, 0, GRID_W - NA_COLS)
    valid = (kc[None, :] >= cs[:, None]) & (kc[None, :] < cs[:, None] + NA_COLS)
    dcol = jnp.clip(kc[None, :] - qc[:, None] + (NA_COLS - 1), 0, 2 * NA_COLS - 2)
    drow = kr[None, :] - shift[:, None] + (NA_ROWS - 1)
    row_pick = (drow[:, :, None] == jnp.arange(2 * NA_ROWS - 1)).astype(F32)
    col_pick = (dcol[:, :, None] == jnp.arange(2 * NA_COLS - 1)).astype(F32)
    hp = lax.Precision.HIGHEST
    t1 = jnp.einsum('hij,ski->hskj', rpb.astype(F32), row_pick, precision=hp)
    tbl = jnp.einsum('hskj,qcj->hsqkc', t1, col_pick, precision=hp)
    tbl = jnp.where(valid[None, None, :, None, :], tbl, NEG_INF)
    tbl = tbl.reshape(H // 2, 2, NA_ROWS, GRID_W, NA_ROWS * GRID_W).transpose(0, 2, 1, 3, 4)
    return tbl.reshape(H // 2, NA_ROWS, 2 * GRID_W, NA_ROWS * GRID_W)


def _na_kernel(q_ref, k_ref, v_ref, tbl_ref, o_ref, s_buf, p_buf, *, rows, rpu):
    head0 = _head0_mask()
    n_units = rows // rpu

    def geom(r):
        rs = _clip(r - NA_ROWS // 2, 0, rows - NA_ROWS)
        qrows = pl.ds(_aligned(r * GRID_W, GRID_W), GRID_W)
        kwin = pl.ds(_aligned(rs * GRID_W, GRID_W), NA_ROWS * GRID_W)
        return r - rs, qrows, kwin

    def stage_a(u):
        q2s, ks, biases = [], [], []
        for t in range(rpu):
            shift, qrows, kwin = geom(u * rpu + t)
            q = q_ref[qrows, :]
            zero = jnp.zeros_like(q)
            q2s.append(jnp.concatenate([jnp.where(head0, q, zero), jnp.where(head0, zero, q)], axis=0))
            ks.append(k_ref[kwin, :])
            biases.append(tbl_ref[shift])
        s = jnp.einsum('bqd,bkd->bqk', jnp.stack(q2s), jnp.stack(ks), preferred_element_type=F32)
        for t in range(rpu):
            s_buf[u % 2, t] = s[t] + biases[t]

    def stage_b(u):
        for t in range(rpu):
            s = s_buf[u % 2, t]
            p = jnp.exp(s - jnp.max(s, axis=-1, keepdims=True))
            p_buf[u % 2, t] = (p * (1.0 / jnp.sum(p, axis=-1, keepdims=True))).astype(BF16)

    def stage_c(u):
        geo = [geom(u * rpu + t) for t in range(rpu)]
        vs = jnp.stack([v_ref[kwin, :] for _, _, kwin in geo])
        o = jnp.einsum('bqk,bkd->bqd', p_buf[u % 2], vs, preferred_element_type=F32)
        for t in range(rpu):
            o_ref[geo[t][1], :] = jnp.where(head0, o[t, :GRID_W], o[t, GRID_W:])

    stage_a(0)
    stage_a(1)
    stage_b(0)

    def body(u, carry):
        stage_c(u)
        stage_b(u + 1)
        stage_a(u + 2)
        return carry

    lax.fori_loop(0, n_units - 2, body, 0, unroll=2)
    stage_c(n_units - 2)
    stage_b(n_units - 1)
    stage_c(n_units - 1)


def _neighbourhood(qb, kb, vb, tbl, rpu=2):
    B, P, T, _ = qb.shape
    rows = T // GRID_W
    assert T % GRID_W == 0 and rows >= NA_ROWS and rows % rpu == 0 and rows // rpu >= 3
    spec = pl.BlockSpec((None, None, T, LANES), lambda b, p: (b, p, 0, 0))
    return pl.pallas_call(
        functools.partial(_na_kernel, rows=rows, rpu=rpu),
        grid=(B, P),
        in_specs=[spec, spec, spec,
                  pl.BlockSpec((None, NA_ROWS, 2 * GRID_W, NA_ROWS * GRID_W), lambda b, p: (p, 0, 0, 0))],
        out_specs=spec,
        out_shape=jax.ShapeDtypeStruct((B, P, T, LANES), F32),
        scratch_shapes=[pltpu.VMEM((2, rpu, 2 * GRID_W, NA_ROWS * GRID_W), F32),
                        pltpu.VMEM((2, rpu, 2 * GRID_W, NA_ROWS * GRID_W), BF16)],
        compiler_params=_cparams(("parallel", "parallel")),
        name="nbr",
    )(qb, kb, vb, tbl)


def _rms(a, g):
    ms = jnp.mean(a * a, axis=-1, keepdims=True)
    return a * lax.rsqrt(ms + EPS) * g


def _post_kernel(od_ref, on_ref, x_ref, wo_ref, gd_ref, gn_ref, gf_ref, wrh_ref, wrl_ref,
                 x1_ref, h_ref, aff_ref):
    dil = jnp.concatenate([od_ref[p] for p in range(PAIRS)], axis=-1)
    nbr = jnp.concatenate([on_ref[p] for p in range(PAIRS)], axis=-1)
    mixed = jnp.concatenate([_rms(dil, gd_ref[...]), _rms(nbr, gn_ref[...])], axis=-1).astype(BF16)
    x1 = x_ref[...] + jnp.dot(mixed, wo_ref[...], preferred_element_type=F32)
    x1_ref[...] = x1
    h = _rms(x1, gf_ref[...])
    hh = h.astype(BF16)
    hl = (h - hh.astype(F32)).astype(BF16)
    h_ref[...] = hh
    wrh = wrh_ref[...]
    lt = (lax.dot_general(wrh, hh, _NT, preferred_element_type=F32)
          + lax.dot_general(wrh, hl, _NT, preferred_element_type=F32)
          + lax.dot_general(wrl_ref[...], hh, _NT, preferred_element_type=F32))
    m = jnp.max(lt, axis=0, keepdims=True)
    e = jnp.exp(lt - m)
    aff_ref[...] = e / jnp.sum(e, axis=0, keepdims=True)


def _post(od, on, x, wo_bf16, gd, gn, gf, wr_hi, wr_lo, tm=512):
    B, T, D = x.shape
    tm = min(tm, T)
    E = wr_hi.shape[0]
    slab = pl.BlockSpec((None, PAIRS, tm, LANES), lambda b, i: (b, 0, i, 0))
    tok = pl.BlockSpec((None, tm, D), lambda b, i: (b, i, 0))
    full = lambda shape: pl.BlockSpec(shape, lambda b, i: (0,) * len(shape))
    return pl.pallas_call(
        _post_kernel,
        grid=(B, T // tm),
        in_specs=[slab, slab, tok, full((2 * GROUP, D)), full((1, GROUP)), full((1, GROUP)),
                  full((1, D)), full((E, D)), full((E, D))],
        out_specs=[tok, tok, pl.BlockSpec((None, E, tm), lambda b, i: (b, 0, i))],
        out_shape=[jax.ShapeDtypeStruct((B, T, D), F32), jax.ShapeDtypeStruct((B, T, D), BF16),
                   jax.ShapeDtypeStruct((B, E, T), F32)],
        compiler_params=_cparams(("parallel", "parallel")),
        name="post",
    )(od, on, x, wo_bf16, gd.reshape(1, GROUP), gn.reshape(1, GROUP), gf.reshape(1, D), wr_hi, wr_lo)


def _select_kernel(aff_ref, pos_ref, pref_ref, *, T, C):
    E = aff_ref.shape[0]
    bits = pltpu.bitcast(aff_ref[...], I32)
    t = jnp.zeros((E, 1), I32)
    for bit in range(30, -1, -1):
        cand = t | (1 << bit)
        cnt = jnp.sum(jnp.where(bits >= cand, 1.0, 0.0), axis=1, keepdims=True)
        t = jnp.where(cnt >= C, cand, t)
    gt = bits > t
    eq = bits == t
    need = C - jnp.sum(jnp.where(gt, 1.0, 0.0), axis=1, keepdims=True)
    tri = jnp.where(lax.broadcasted_iota(I32, (LANES, LANES), 0) < lax.broadcasted_iota(I32, (LANES, LANES), 1),
                    1.0, 0.0).astype(BF16)
    eq_carry = jnp.zeros((E, 1), F32)
    sel_carry = jnp.zeros((E, 1), F32)
    for c in range(T // LANES):
        sl = slice(c * LANES, (c + 1) * LANES)
        eqc = jnp.where(eq[:, sl], 1.0, 0.0)
        eq_rank = jnp.dot(eqc.astype(BF16), tri, preferred_element_type=F32) + eq_carry
        eq_carry = eq_carry + jnp.sum(eqc, axis=1, keepdims=True)
        sel = jnp.logical_or(gt[:, sl], jnp.logical_and(eq[:, sl], eq_rank < need))
        selc = jnp.where(sel, 1.0, 0.0)
        pref = (jnp.dot(selc.astype(BF16), tri, preferred_element_type=F32) + sel_carry).astype(I32)
        sel_carry = sel_carry + jnp.sum(selc, axis=1, keepdims=True)
        pref_ref[:, sl] = pref
        pos_ref[:, sl] = jnp.where(sel, pref, -1)


def _select(aff_t, C):
    B, E, T = aff_t.shape
    spec = pl.BlockSpec((None, E, T), lambda b: (b, 0, 0))
    return pl.pallas_call(
        functools.partial(_select_kernel, T=T, C=C),
        grid=(B,),
        in_specs=[spec],
        out_specs=[spec, spec],
        out_shape=[jax.ShapeDtypeStruct((B, E, T), I32), jax.ShapeDtypeStruct((B, E, T), I32)],
        compiler_params=_cparams(("parallel",)),
        name="select",
    )(aff_t)


def _count_below(ref, start, n, bound):
    lo = jnp.int32(0)
    step = 1 << (n.bit_length() - 1)
    while step:
        probe = lo + step
        ok = jnp.logical_and(probe <= n, ref[start + jnp.minimum(probe, n) - 1] < bound)
        lo = jnp.where(ok, probe, lo)
        step >>= 1
    return lo


def _dispatch_kernel(cc_ref, pos_ref, aff_ref, h_ref, xg_ref, gate_ref, acc_ref, gacc_ref, *, C, Cs, Tc, NC, KW):
    E = pl.num_programs(1)
    row = (pl.program_id(0) * E + pl.program_id(1)) * (NC + 1)
    for j in range(C // Cs):
        base = j * Cs

        c_lo = _count_below(cc_ref, row + 1, NC, base + 1)
        c_hi = _count_below(cc_ref, row, NC, base + Cs)
        slot = base + lax.broadcasted_iota(I32, (Cs, Tc), 0)
        c0 = jnp.minimum(c_lo, NC - KW)
        pw = pos_ref[pl.ds(c0, KW), :]
        aw = aff_ref[pl.ds(c0, KW), :]
        hits = [pw[kk:kk + 1, :] == slot for kk in range(KW)]
        onehot = jnp.concatenate([jnp.where(hit, 1.0, 0.0).astype(BF16) for hit in hits], axis=1)
        hwin = h_ref[pl.ds(pl.multiple_of(c0 * Tc, Tc), KW * Tc), :]
        acc_ref[...] = jnp.dot(onehot, hwin, preferred_element_type=F32)
        gsum = jnp.where(hits[0], aw[0:1, :], 0.0)
        for kk in range(1, KW):
            gsum = gsum + jnp.where(hits[kk], aw[kk:kk + 1, :], 0.0)
        gacc_ref[...] = jnp.sum(gsum, axis=1, keepdims=True)

        def chunk(c, carry, slot=slot):
            hit = pos_ref[pl.ds(c, 1), :] == slot
            hc = h_ref[pl.ds(pl.multiple_of(c * Tc, Tc), Tc), :]
            acc_ref[...] += jnp.dot(jnp.where(hit, 1.0, 0.0).astype(BF16), hc, preferred_element_type=F32)
            gacc_ref[...] += jnp.sum(jnp.where(hit, aff_ref[pl.ds(c, 1), :], 0.0), axis=1, keepdims=True)
            return carry

        lax.fori_loop(c0 + KW, c_hi, chunk, 0)
        xg_ref[base:base + Cs, :] = acc_ref[...].astype(BF16)
        gate_ref[base:base + Cs, :] = jnp.broadcast_to(gacc_ref[...], (Cs, LANES))


def _dispatch(cc, pos, aff_t, h, C, Cs=128, Tc=LANES, KW=12):
    B, E, T = pos.shape
    D = h.shape[-1]
    NC = T // Tc
    assert T % Tc == 0 and C % Cs == 0 and KW <= NC
    row_spec = pl.BlockSpec((None, None, NC, Tc), lambda b, e, cc: (b, e, 0, 0))
    return pl.pallas_call(
        functools.partial(_dispatch_kernel, C=C, Cs=Cs, Tc=Tc, NC=NC, KW=KW),
        grid_spec=pltpu.PrefetchScalarGridSpec(
            num_scalar_prefetch=1,
            grid=(B, E),
            in_specs=[row_spec, row_spec,
                      pl.BlockSpec((None, T, D), lambda b, e, cc: (b, 0, 0), pipeline_mode=pl.Buffered(1))],
            out_specs=[pl.BlockSpec((None, None, C, D), lambda b, e, cc: (e, b, 0, 0)),
                       pl.BlockSpec((None, None, C, LANES), lambda b, e, cc: (e, b, 0, 0))],
            scratch_shapes=[pltpu.VMEM((Cs, D), F32), pltpu.VMEM((Cs, 1), F32)],
        ),
        out_shape=[jax.ShapeDtypeStruct((E, B, C, D), BF16), jax.ShapeDtypeStruct((E, B, C, LANES), F32)],
        compiler_params=_cparams(("arbitrary", "arbitrary")),
        name="dispatch",
    )(cc, pos.reshape(B, E, NC, Tc), aff_t.reshape(B, E, NC, Tc), h)


def _ffn_kernel(x_ref, wg_ref, wu_ref, wd_ref, gate_ref, y_ref, acc_ref, *, FF, fc):
    x = x_ref[...]
    for kc in range(FF // fc):
        sl = slice(kc * fc, (kc + 1) * fc)
        g = jnp.dot(x, wg_ref[:, sl], preferred_element_type=F32)
        u = jnp.dot(x, wu_ref[:, sl], preferred_element_type=F32)
        a = (g * jax.nn.sigmoid(g) * u).astype(BF16)
        y = jnp.dot(a, wd_ref[sl, :], preferred_element_type=F32)
        if kc == 0:
            acc_ref[...] = y
        else:
            acc_ref[...] += y
    gate = gate_ref[...]
    for s in range(y_ref.shape[-1] // LANES):
        sl = slice(s * LANES, (s + 1) * LANES)
        y_ref[:, sl] = (acc_ref[:, sl] * gate).astype(y_ref.dtype)


def _ffn(xg, wg, wu, wd, gates, layer, tm=1024, fc=256):
    E, R, D = xg.shape
    FF = wg.shape[-1]
    tm = min(tm, R)
    assert R % tm == 0 and FF % fc == 0
    return pl.pallas_call(
        functools.partial(_ffn_kernel, FF=FF, fc=fc),
        grid=(E, R // tm),
        in_specs=[pl.BlockSpec((None, tm, D), lambda e, i: (e, i, 0)),
                  pl.BlockSpec((None, None, D, FF), lambda e, i: (layer, e, 0, 0)),
                  pl.BlockSpec((None, None, D, FF), lambda e, i: (layer, e, 0, 0)),
                  pl.BlockSpec((None, None, FF, D), lambda e, i: (layer, e, 0, 0)),
                  pl.BlockSpec((None, tm, LANES), lambda e, i: (e, i, 0))],
        out_specs=pl.BlockSpec((None, tm, D), lambda e, i: (e, i, 0)),
        out_shape=jax.ShapeDtypeStruct((E, R, D), BF16),
        scratch_shapes=[pltpu.VMEM((tm, D), F32)],
        compiler_params=_cparams(("parallel", "parallel")),
        name="ffn",
    )(xg, wg, wu, wd, gates)


def _combine_kernel(ss_ref, pos_ref, x1_ref, ye_ref, gfin_ref, out_ref, acc_ref, yall_ref, *, C, W, NT, final):
    E = ye_ref.shape[0]
    tm = x1_ref.shape[0]
    b = pl.program_id(0)
    i = pl.program_id(1)
    starts = []
    for e in range(E):
        row = (b * E + e) * (NT + 1) + i
        s0 = ss_ref[row]
        s1 = ss_ref[row + 1]
        a = pl.multiple_of(jnp.minimum((s0 // 16) * 16, C - W), 16)
        starts.append((a, jnp.maximum((s1 - a - 1) // W, 0)))
    lane = lax.broadcasted_iota(I32, (tm, 2 * W), 1)
    first = lane < W
    acc = x1_ref[...]
    for g in range(0, E, GROUP_E):
        pieces = []
        for e in range(g, g + GROUP_E, 2):
            for ee in (e, e + 1):
                yall_ref[ee * W:(ee + 1) * W, :] = ye_ref[ee, pl.ds(starts[ee][0], W), :]
            slot = jnp.where(first, starts[e][0] + lane, starts[e + 1][0] + lane - W)
            pcol = jnp.where(first, pos_ref[:, e:e + 1], pos_ref[:, e + 1:e + 2])
            pieces.append(jnp.where(pcol == slot, 1.0, 0.0).astype(BF16))
        onehot = jnp.concatenate(pieces, axis=1)
        acc = acc + jnp.dot(onehot, yall_ref[g * W:(g + GROUP_E) * W, :], preferred_element_type=F32)
    acc_ref[...] = acc
    n_extra_all = starts[0][1]
    for e in range(1, E):
        n_extra_all = n_extra_all + starts[e][1]

    @pl.when(n_extra_all > 0)
    def _():
        lane_w = lax.broadcasted_iota(I32, (tm, W), 1)
        for e in range(E):
            a0, n_extra = starts[e]
            pcol = pos_ref[:, e:e + 1]

            def chunk(k, carry, e=e, a0=a0, pcol=pcol):
                lo = a0 + k * W
                a = pl.multiple_of(jnp.minimum(lo, C - W), 16)
                slot = a + lane_w
                hit = jnp.logical_and(pcol == slot, slot >= lo)
                y = ye_ref[e, pl.ds(a, W), :]
                acc_ref[...] += jnp.dot(jnp.where(hit, 1.0, 0.0).astype(BF16), y, preferred_element_type=F32)
                return carry

            lax.fori_loop(1, n_extra + 1, chunk, 0)

    out = acc_ref[...]
    if final:
        out = _rms(out, gfin_ref[...])
    out_ref[...] = out


def _combine(ss, pos_t, x1, ye, gfin, C, final, tm=256, W=64):
    B, T, D = x1.shape
    E = ye.shape[0]
    tm = min(tm, T)
    NT = T // tm
    assert C % 16 == 0 and W % 16 == 0 and W <= C
    return pl.pallas_call(
        functools.partial(_combine_kernel, C=C, W=W, NT=NT, final=final),
        grid_spec=pltpu.PrefetchScalarGridSpec(
            num_scalar_prefetch=1,
            grid=(B, NT),
            in_specs=[pl.BlockSpec((None, tm, E), lambda b, i, ss: (b, i, 0)),
                      pl.BlockSpec((None, tm, D), lambda b, i, ss: (b, i, 0)),
                      pl.BlockSpec((E, None, C, D), lambda b, i, ss: (0, b, 0, 0), pipeline_mode=pl.Buffered(1)),
                      pl.BlockSpec((1, D), lambda b, i, ss: (0, 0))],
            out_specs=pl.BlockSpec((None, tm, D), lambda b, i, ss: (b, i, 0)),
            scratch_shapes=[pltpu.VMEM((tm, D), F32), pltpu.VMEM((E * W, D), BF16)],
        ),
        out_shape=jax.ShapeDtypeStruct((B, T, D), F32),
        compiler_params=_cparams(("arbitrary", "arbitrary")),
        name="combine",
    )(ss, pos_t, x1, ye, gfin.reshape(1, D))


def _rope_tables(T):
    pos = jnp.arange(T, dtype=F32)
    inv = ROPE_THETA ** (-jnp.arange(0, HEAD_DIM, 2, dtype=F32) / HEAD_DIM)
    ang = pos[:, None] * inv[None, :]
    cos = jnp.tile(jnp.cos(ang), (1, LANES // (HEAD_DIM // 2)))
    sin = jnp.sin(ang)
    sin = jnp.tile(jnp.concatenate([-sin, sin], axis=-1), (1, LANES // HEAD_DIM))
    return cos, sin


def _chunk_starts(pref, step, total):
    B, E, _ = pref.shape
    tail = jnp.full((B, E, 1), total, I32)
    return jnp.concatenate([pref[:, :, ::step], tail], axis=-1).reshape(-1)


def kernel(x, attn_norm, w_in, dil_out_norm, na_out_norm, na_rpb, w_out, ffn_norm, w_router,
           w_gate, w_up, w_down, final_norm):
    B, T, D = x.shape
    depth = w_in.shape[0]
    C = EC_CAPACITY * T // N_EXPERTS
    cos, sin = _rope_tables(T)
    tc_dispatch, tm_combine = LANES, min(256, T)
    wg_bf, wu_bf, wd_bf = w_gate.astype(BF16), w_up.astype(BF16), w_down.astype(BF16)
    for l in range(depth):
        qa, ka, va, qb, kb, vb = _qkv(x, attn_norm[l], w_in[l].astype(BF16), cos, sin)
        od = _dilated(qa, ka, va)
        on = _neighbourhood(qb, kb, vb, _na_bias_tables(na_rpb[l]))
        wr_t = w_router[l].T
        wr_hi = wr_t.astype(BF16)
        wr_lo = (wr_t - wr_hi.astype(F32)).astype(BF16)
        x1, h, aff_t = _post(od, on, x, w_out[l].astype(BF16), dil_out_norm[l], na_out_norm[l],
                             ffn_norm[l], wr_hi, wr_lo)
        pos, pref = _select(aff_t, C)
        xg, gates = _dispatch(_chunk_starts(pref, tc_dispatch, C), pos, aff_t, h, C, Tc=tc_dispatch)
        ye = _ffn(xg.reshape(N_EXPERTS, B * C, D), wg_bf, wu_bf, wd_bf,
                  gates.reshape(N_EXPERTS, B * C, LANES), l)
        x = _combine(_chunk_starts(pref, tm_combine, C), pos.transpose(0, 2, 1), x1,
                     ye.reshape(N_EXPERTS, B, C, D), final_norm, C, final=(l == depth - 1), tm=tm_combine)
    return x
```

```python
import functools

import jax
import jax.numpy as jnp
from jax import lax
from jax.experimental import pallas as pl
from jax.experimental.pallas import tpu as pltpu

F32 = jnp.float32
BF16 = jnp.bfloat16
I32 = jnp.int32

HEAD_DIM = 64
LANES = 128
PAIRS = 4
GROUP = PAIRS * LANES
ATTN_SCALE = HEAD_DIM ** -0.5
ROPE_THETA = 10000.0
DIL_PATTERNS = ((128, 1), (512, 4), (2048, 16))
GRID_W = 64
NA_ROWS = 8
NA_COLS = 16
N_EXPERTS = 16
GROUP_E = 4
EC_CAPACITY = 2
EPS = 1e-6
NEG_INF = -1e30

VMEM_LIMIT = 56 * 1024 * 1024

_NT = (((1,), (1,)), ((), ()))


def _cparams(sem, vmem=VMEM_LIMIT):
    return pltpu.CompilerParams(dimension_semantics=sem, vmem_limit_bytes=vmem)


def _head0_mask():
    return lax.broadcasted_iota(I32, (1, LANES), 1) < HEAD_DIM


def _clip(x, lo, hi):
    return min(max(x, lo), hi) if isinstance(x, int) else jnp.clip(x, lo, hi)


def _aligned(x, m):
    return x if isinstance(x, int) else pl.multiple_of(x, m)


def _qkv_kernel(x_ref, g_ref, w_ref, cos_ref, sin_ref, *refs):
    n_dil = len(DIL_PATTERNS)
    dil_refs = [refs[t * n_dil:(t + 1) * n_dil] for t in range(3)]
    nbr_refs = refs[3 * n_dil:3 * n_dil + 3]
    ys_ref = refs[-1]
    tm = x_ref.shape[0]
    x = x_ref[...]
    ms = jnp.mean(x * x, axis=-1, keepdims=True)
    h = (x * lax.rsqrt(ms + EPS) * g_ref[...]).astype(BF16)
    cos = cos_ref[...]
    sin = sin_ref[...]
    lane = lax.broadcasted_iota(I32, (1, LANES), 1)
    first_half = (lane % HEAD_DIM) < HEAD_DIM // 2
    for gi in range(6):
        y = jnp.dot(h, w_ref[:, gi * GROUP:(gi + 1) * GROUP], preferred_element_type=F32)
        for p in range(PAIRS):
            s = y[:, p * LANES:(p + 1) * LANES]
            if gi < 2:
                partner = jnp.where(first_half, pltpu.roll(s, LANES - HEAD_DIM // 2, 1),
                                    pltpu.roll(s, HEAD_DIM // 2, 1))
                s = s * cos + partner * sin
            if gi in (0, 3):
                s = s * ATTN_SCALE
            if gi >= 3:
                nbr_refs[gi - 3][p] = s.astype(BF16)
                continue
            ys_ref[gi * PAIRS + p] = s
            for (_, d), o_ref in zip(DIL_PATTERNS, dil_refs[gi]):
                if d == 1:
                    o_ref[p, 0] = s.astype(BF16)
    for t in range(3):
        for (_, d), o_ref in zip(DIL_PATTERNS, dil_refs[t]):
            if d == 1:
                continue

            def regroup(p, carry, t=t, d=d, o_ref=o_ref):
                for r in range(d):
                    o_ref[p, r] = ys_ref[t * PAIRS + p, pl.ds(r, tm // d, stride=d), :].astype(BF16)
                return carry

            lax.fori_loop(0, PAIRS, regroup, 0)


def _qkv(x, g, w_bf16, cos, sin, tm=512):
    B, T, D = x.shape
    tm = min(tm, T)
    dils = [d for _, d in DIL_PATTERNS]
    assert all(tm % (16 * d) == 0 for d in dils), "bf16 rows travel in packs of 16"
    dil_shapes = [jax.ShapeDtypeStruct((B, PAIRS, d, T // d, LANES), BF16) for d in dils]
    dil_specs = [pl.BlockSpec((None, PAIRS, d, tm // d, LANES), lambda b, i: (b, 0, 0, i, 0)) for d in dils]
    nbr_shape = jax.ShapeDtypeStruct((B, PAIRS, T, LANES), BF16)
    nbr_spec = pl.BlockSpec((None, PAIRS, tm, LANES), lambda b, i: (b, 0, i, 0))
    outs = pl.pallas_call(
        _qkv_kernel,
        grid=(B, T // tm),
        in_specs=[
            pl.BlockSpec((None, tm, D), lambda b, i: (b, i, 0)),
            pl.BlockSpec((1, D), lambda b, i: (0, 0)),
            pl.BlockSpec((D, 6 * GROUP), lambda b, i: (0, 0)),
            pl.BlockSpec((tm, LANES), lambda b, i: (i, 0)),
            pl.BlockSpec((tm, LANES), lambda b, i: (i, 0)),
        ],
        out_specs=dil_specs * 3 + [nbr_spec] * 3,
        out_shape=dil_shapes * 3 + [nbr_shape] * 3,
        scratch_shapes=[pltpu.VMEM((3 * PAIRS, tm, LANES), F32)],
        compiler_params=_cparams(("parallel", "parallel")),
        name="qkv",
    )(x, g.reshape(1, D), w_bf16, cos, sin)
    n = len(dils)
    return (tuple(outs[:n]), tuple(outs[n:2 * n]), tuple(outs[2 * n:3 * n])) + tuple(outs[3 * n:])


def _dil_kernel(*refs, T, Lq, unroll):
    n_br = len(DIL_PATTERNS)
    q_refs, k_refs, v_refs = refs[:n_br], refs[n_br:2 * n_br], refs[2 * n_br:3 * n_br]
    o_ref, lse_ref, bias_ref = refs[3 * n_br:]
    head0 = _head0_mask()
    for bi, (window, d) in enumerate(DIL_PATTERNS):
        q_ref, k_ref, v_ref = q_refs[bi], k_refs[bi], v_refs[bi]
        half = window // (2 * d)
        L = T // d
        Wn = Lq + 2 * half
        nb = L // Lq
        rel0 = (lax.broadcasted_iota(I32, (Lq, Wn), 1) - lax.broadcasted_iota(I32, (Lq, Wn), 0))
        for ci, off in enumerate((0, -half, -2 * half)):
            bias_ref[ci] = jnp.where(jnp.abs(rel0 + off) <= half, 0.0, NEG_INF)

        def group(gi, carry, bi=bi, d=d, half=half, L=L, Wn=Wn, nb=nb):
            geo, q2s, ks, vs, biases = [], [], [], [], []
            for t in range(unroll):
                idx = gi * unroll + t
                r = idx // nb
                n = idx % nb
                m0 = n * Lq
                ws = jnp.clip(m0 - half, 0, L - Wn)

                q = q_ref[r, pl.ds(pl.multiple_of(m0, Lq), Lq), :]
                kvrows = pl.ds(pl.multiple_of(ws, half), Wn)
                ks.append(k_ref[r, kvrows, :])
                vs.append(v_ref[r, kvrows, :])
                biases.append(bias_ref[jnp.where(n == 0, 0, jnp.where(n == nb - 1, 2, 1))])
                zero = jnp.zeros_like(q)
                q2s.append(jnp.concatenate([jnp.where(head0, q, zero), jnp.where(head0, zero, q)], axis=0))
                geo.append(pl.ds(m0, Lq) if d == 1 else pl.ds(r + d * m0, Lq, stride=d))
            s = jnp.einsum('bqd,bkd->bqk', jnp.stack(q2s), jnp.stack(ks), preferred_element_type=F32)
            ps, stats = [], []
            for t in range(unroll):
                st = []
                for hs in (s[t, :Lq], s[t, Lq:]):
                    hs = hs + biases[t]
                    mh = jnp.max(hs, axis=-1, keepdims=True)
                    p = jnp.exp(hs - mh)
                    st.append((mh, jnp.sum(p, axis=-1, keepdims=True), p.astype(BF16)))
                ps.append(jnp.concatenate([st[0][2], st[1][2]], axis=0))
                stats.append(st)
            pv = jnp.einsum('bqk,bkd->bqd', jnp.stack(ps), jnp.stack(vs), preferred_element_type=F32)
            for t in range(unroll):
                st = stats[t]
                lb = jnp.where(head0, st[0][1], st[1][1])
                lse_b = jnp.where(head0, st[0][0], st[1][0]) + jnp.log(lb)
                ob = jnp.where(head0, pv[t, :Lq], pv[t, Lq:]) / lb
                qrows = geo[t]
                if bi == 0:
                    lse_ref[qrows, :] = lse_b
                    o_ref[qrows, :] = ob
                else:
                    lse_o = lse_ref[qrows, :]
                    mn = jnp.maximum(lse_o, lse_b)
                    wo = jnp.exp(lse_o - mn)
                    wb = jnp.exp(lse_b - mn)
                    den = wo + wb
                    o_ref[qrows, :] = (wo * o_ref[qrows, :] + wb * ob) / den
                    if bi < n_br - 1:
                        lse_ref[qrows, :] = mn + jnp.log(den)
            return carry

        lax.fori_loop(0, d * nb // unroll, group, 0)


def _dilated(qs, ks, vs, Lq=128, unroll=4):
    B, P = qs[0].shape[:2]
    T = qs[0].shape[2] * qs[0].shape[3]
    halves = {window // (2 * d) for window, d in DIL_PATTERNS}
    assert len(halves) == 1, "the band-mask scratch is sized for one half-width"
    half = halves.pop()
    assert half % 16 == 0, "bf16 rows travel in packs of 16"
    for window, d in DIL_PATTERNS:
        L = T // d
        assert T % d == 0 and L % Lq == 0 and L // Lq >= 2 and Lq >= half and Lq + 2 * half <= L, (T, window, d)
        assert (d * (L // Lq)) % unroll == 0
    in_specs = [pl.BlockSpec((None, None, d, T // d, LANES), lambda b, p: (b, p, 0, 0, 0))
                for _ in range(3) for _, d in DIL_PATTERNS]
    out_spec = pl.BlockSpec((None, None, T, LANES), lambda b, p: (b, p, 0, 0))
    return pl.pallas_call(
        functools.partial(_dil_kernel, T=T, Lq=Lq, unroll=unroll),
        grid=(B, P),
        in_specs=in_specs,
        out_specs=out_spec,
        out_shape=jax.ShapeDtypeStruct((B, P, T, LANES), F32),
        scratch_shapes=[pltpu.VMEM((T, LANES), F32), pltpu.VMEM((3, Lq, Lq + 2 * half), F32)],
        compiler_params=_cparams(("parallel", "parallel")),
        name="dilated",
    )(*qs, *ks, *vs)


def _na_bias_tables(rpb):
    H = rpb.shape[0]
    qc = jnp.arange(GRID_W)
    kc = jnp.arange(GRID_W)
    kr = jnp.arange(NA_ROWS)
    shift = jnp.arange(NA_ROWS)
    cs = jnp.clip(qc - NA_COLS // 2, 0, GRID_W - NA_COLS)
    valid = (kc[None, :] >= cs[:, None]) & (kc[None, :] < cs[:, None] + NA_COLS)
    dcol = jnp.clip(kc[None, :] - qc[:, None] + (NA_COLS - 1), 0, 2 * NA_COLS - 2)
    drow = kr[None, :] - shift[:, None] + (NA_ROWS - 1)
    row_pick = (drow[:, :, None] == jnp.arange(2 * NA_ROWS - 1)).astype(F32)
    col_pick = (dcol[:, :, None] == jnp.arange(2 * NA_COLS - 1)).astype(F32)
    hp = lax.Precision.HIGHEST
    t1 = jnp.einsum('hij,ski->hskj', rpb.astype(F32), row_pick, precision=hp)
    tbl = jnp.einsum('hskj,qcj->hsqkc', t1, col_pick, precision=hp)
    tbl = jnp.where(valid[None, None, :, None, :], tbl, NEG_INF)
    tbl = tbl.reshape(H // 2, 2, NA_ROWS, GRID_W, NA_ROWS * GRID_W).transpose(0, 2, 1, 3, 4)
    return tbl.reshape(H // 2, NA_ROWS, 2 * GRID_W, NA_ROWS * GRID_W)


def _na_kernel(q_ref, k_ref, v_ref, tbl_ref, o_ref, s_buf, p_buf, *, rows, rpu):
    head0 = _head0_mask()
    n_units = rows // rpu

    def geom(r):
        rs = _clip(r - NA_ROWS // 2, 0, rows - NA_ROWS)
        qrows = pl.ds(_aligned(r * GRID_W, GRID_W), GRID_W)
        kwin = pl.ds(_aligned(rs * GRID_W, GRID_W), NA_ROWS * GRID_W)
        return r - rs, qrows, kwin

    def stage_a(u):
        q2s, ks, biases = [], [], []
        for t in range(rpu):
            shift, qrows, kwin = geom(u * rpu + t)
            q = q_ref[qrows, :]
            zero = jnp.zeros_like(q)
            q2s.append(jnp.concatenate([jnp.where(head0, q, zero), jnp.where(head0, zero, q)], axis=0))
            ks.append(k_ref[kwin, :])
            biases.append(tbl_ref[shift])
        s = jnp.einsum('bqd,bkd->bqk', jnp.stack(q2s), jnp.stack(ks), preferred_element_type=F32)
        for t in range(rpu):
            s_buf[u % 2, t] = s[t] + biases[t]

    def stage_b(u):
        for t in range(rpu):
            s = s_buf[u % 2, t]
            p = jnp.exp(s - jnp.max(s, axis=-1, keepdims=True))
            p_buf[u % 2, t] = (p * (1.0 / jnp.sum(p, axis=-1, keepdims=True))).astype(BF16)

    def stage_c(u):
        geo = [geom(u * rpu + t) for t in range(rpu)]
        vs = jnp.stack([v_ref[kwin, :] for _, _, kwin in geo])
        o = jnp.einsum('bqk,bkd->bqd', p_buf[u % 2], vs, preferred_element_type=F32)
        for t in range(rpu):
            o_ref[geo[t][1], :] = jnp.where(head0, o[t, :GRID_W], o[t, GRID_W:])

    stage_a(0)
    stage_a(1)
    stage_b(0)

    def body(u, carry):
        stage_c(u)
        stage_b(u + 1)
        stage_a(u + 2)
        return carry

    lax.fori_loop(0, n_units - 2, body, 0, unroll=2)
    stage_c(n_units - 2)
    stage_b(n_units - 1)
    stage_c(n_units - 1)


def _neighbourhood(qb, kb, vb, tbl, rpu=2):
    B, P, T, _ = qb.shape
    rows = T // GRID_W
    assert T % GRID_W == 0 and rows >= NA_ROWS and rows % rpu == 0 and rows // rpu >= 3
    spec = pl.BlockSpec((None, None, T, LANES), lambda b, p: (b, p, 0, 0))
    return pl.pallas_call(
        functools.partial(_na_kernel, rows=rows, rpu=rpu),
        grid=(B, P),
        in_specs=[spec, spec, spec,
                  pl.BlockSpec((None, NA_ROWS, 2 * GRID_W, NA_ROWS * GRID_W), lambda b, p: (p, 0, 0, 0))],
        out_specs=spec,
        out_shape=jax.ShapeDtypeStruct((B, P, T, LANES), F32),
        scratch_shapes=[pltpu.VMEM((2, rpu, 2 * GRID_W, NA_ROWS * GRID_W), F32),
                        pltpu.VMEM((2, rpu, 2 * GRID_W, NA_ROWS * GRID_W), BF16)],
        compiler_params=_cparams(("parallel", "parallel")),
        name="nbr",
    )(qb, kb, vb, tbl)


def _rms(a, g):
    ms = jnp.mean(a * a, axis=-1, keepdims=True)
    return a * lax.rsqrt(ms + EPS) * g


def _post_kernel(od_ref, on_ref, x_ref, wo_ref, gd_ref, gn_ref, gf_ref, wrh_ref, wrl_ref,
                 x1_ref, h_ref, aff_ref):
    dil = jnp.concatenate([od_ref[p] for p in range(PAIRS)], axis=-1)
    nbr = jnp.concatenate([on_ref[p] for p in range(PAIRS)], axis=-1)
    mixed = jnp.concatenate([_rms(dil, gd_ref[...]), _rms(nbr, gn_ref[...])], axis=-1).astype(BF16)
    x1 = x_ref[...] + jnp.dot(mixed, wo_ref[...], preferred_element_type=F32)
    x1_ref[...] = x1
    h = _rms(x1, gf_ref[...])
    hh = h.astype(BF16)
    hl = (h - hh.astype(F32)).astype(BF16)
    h_ref[...] = hh
    wrh = wrh_ref[...]
    lt = (lax.dot_general(wrh, hh, _NT, preferred_element_type=F32)
          + lax.dot_general(wrh, hl, _NT, preferred_element_type=F32)
          + lax.dot_general(wrl_ref[...], hh, _NT, preferred_element_type=F32))
    m = jnp.max(lt, axis=0, keepdims=True)
    e = jnp.exp(lt - m)
    aff_ref[...] = e / jnp.sum(e, axis=0, keepdims=True)


def _post(od, on, x, wo_bf16, gd, gn, gf, wr_hi, wr_lo, tm=1024):
    B, T, D = x.shape
    tm = min(tm, T)
    E = wr_hi.shape[0]
    slab = pl.BlockSpec((None, PAIRS, tm, LANES), lambda b, i: (b, 0, i, 0))
    tok = pl.BlockSpec((None, tm, D), lambda b, i: (b, i, 0))
    full = lambda shape: pl.BlockSpec(shape, lambda b, i: (0,) * len(shape))
    return pl.pallas_call(
        _post_kernel,
        grid=(B, T // tm),
        in_specs=[slab, slab, tok, full((2 * GROUP, D)), full((1, GROUP)), full((1, GROUP)),
                  full((1, D)), full((E, D)), full((E, D))],
        out_specs=[tok, tok, pl.BlockSpec((None, E, tm), lambda b, i: (b, 0, i))],
        out_shape=[jax.ShapeDtypeStruct((B, T, D), F32), jax.ShapeDtypeStruct((B, T, D), BF16),
                   jax.ShapeDtypeStruct((B, E, T), F32)],
        compiler_params=_cparams(("parallel", "parallel")),
        name="post",
    )(od, on, x, wo_bf16, gd.reshape(1, GROUP), gn.reshape(1, GROUP), gf.reshape(1, D), wr_hi, wr_lo)


def _select_kernel(aff_ref, pos_ref, pref_ref, *, T, C):
    E = aff_ref.shape[0]
    bits = pltpu.bitcast(aff_ref[...], I32)
    t = jnp.zeros((E, 1), I32)
    for bit in range(30, -1, -1):
        cand = t | (1 << bit)
        cnt = jnp.sum(jnp.where(bits >= cand, 1.0, 0.0), axis=1, keepdims=True)
        t = jnp.where(cnt >= C, cand, t)
    gt = bits > t
    eq = bits == t
    need = C - jnp.sum(jnp.where(gt, 1.0, 0.0), axis=1, keepdims=True)
    tri = jnp.where(lax.broadcasted_iota(I32, (LANES, LANES), 0) < lax.broadcasted_iota(I32, (LANES, LANES), 1),
                    1.0, 0.0).astype(BF16)
    eq_carry = jnp.zeros((E, 1), F32)
    sel_carry = jnp.zeros((E, 1), F32)
    for c in range(T // LANES):
        sl = slice(c * LANES, (c + 1) * LANES)
        eqc = jnp.where(eq[:, sl], 1.0, 0.0)
        eq_rank = jnp.dot(eqc.astype(BF16), tri, preferred_element_type=F32) + eq_carry
        eq_carry = eq_carry + jnp.sum(eqc, axis=1, keepdims=True)
        sel = jnp.logical_or(gt[:, sl], jnp.logical_and(eq[:, sl], eq_rank < need))
        selc = jnp.where(sel, 1.0, 0.0)
        pref = (jnp.dot(selc.astype(BF16), tri, preferred_element_type=F32) + sel_carry).astype(I32)
        sel_carry = sel_carry + jnp.sum(selc, axis=1, keepdims=True)
        pref_ref[:, sl] = pref
        pos_ref[:, sl] = jnp.where(sel, pref, -1)


def _select(aff_t, C):
    B, E, T = aff_t.shape
    spec = pl.BlockSpec((None, E, T), lambda b: (b, 0, 0))
    return pl.pallas_call(
        functools.partial(_select_kernel, T=T, C=C),
        grid=(B,),
        in_specs=[spec],
        out_specs=[spec, spec],
        out_shape=[jax.ShapeDtypeStruct((B, E, T), I32), jax.ShapeDtypeStruct((B, E, T), I32)],
        compiler_params=_cparams(("parallel",)),
        name="select",
    )(aff_t)


def _count_below(ref, start, n, bound):
    lo = jnp.int32(0)
    step = 1 << (n.bit_length() - 1)
    while step:
        probe = lo + step
        ok = jnp.logical_and(probe <= n, ref[start + jnp.minimum(probe, n) - 1] < bound)
        lo = jnp.where(ok, probe, lo)
        step >>= 1
    return lo


def _dispatch_kernel(cc_ref, pos_ref, aff_ref, h_ref, wg_ref, wu_ref, wd_ref,
                     xg_ref, gate_ref, wgo_ref, wuo_ref, wdo_ref, acc_ref, gacc_ref, *, C, Cs, Tc, NC, KW):
    wgo_ref[...] = wg_ref[...].astype(BF16)
    wuo_ref[...] = wu_ref[...].astype(BF16)
    wdo_ref[...] = wd_ref[...].astype(BF16)
    E = pl.num_programs(1)
    row = (pl.program_id(0) * E + pl.program_id(1)) * (NC + 1)
    for j in range(C // Cs):
        base = j * Cs

        c_lo = _count_below(cc_ref, row + 1, NC, base + 1)
        c_hi = _count_below(cc_ref, row, NC, base + Cs)
        slot = base + lax.broadcasted_iota(I32, (Cs, Tc), 0)
        c0 = jnp.minimum(c_lo, NC - KW)
        pw = pos_ref[pl.ds(c0, KW), :]
        aw = aff_ref[pl.ds(c0, KW), :]
        hits = [pw[kk:kk + 1, :] == slot for kk in range(KW)]
        onehot = jnp.concatenate([jnp.where(hit, 1.0, 0.0).astype(BF16) for hit in hits], axis=1)
        hwin = h_ref[pl.ds(pl.multiple_of(c0 * Tc, Tc), KW * Tc), :]
        acc_ref[...] = jnp.dot(onehot, hwin, preferred_element_type=F32)
        gsum = jnp.where(hits[0], aw[0:1, :], 0.0)
        for kk in range(1, KW):
            gsum = gsum + jnp.where(hits[kk], aw[kk:kk + 1, :], 0.0)
        gacc_ref[...] = jnp.sum(gsum, axis=1, keepdims=True)

        def chunk(c, carry, slot=slot):
            hit = pos_ref[pl.ds(c, 1), :] == slot
            hc = h_ref[pl.ds(pl.multiple_of(c * Tc, Tc), Tc), :]
            acc_ref[...] += jnp.dot(jnp.where(hit, 1.0, 0.0).astype(BF16), hc, preferred_element_type=F32)
            gacc_ref[...] += jnp.sum(jnp.where(hit, aff_ref[pl.ds(c, 1), :], 0.0), axis=1, keepdims=True)
            return carry

        lax.fori_loop(c0 + KW, c_hi, chunk, 0)
        xg_ref[base:base + Cs, :] = acc_ref[...].astype(BF16)
        gate_ref[base:base + Cs, :] = jnp.broadcast_to(gacc_ref[...], (Cs, LANES))


def _dispatch(cc, pos, aff_t, h, C, weights, layer, Cs=128, Tc=LANES, KW=12):
    B, E, T = pos.shape
    D = h.shape[-1]
    NC = T // Tc
    assert T % Tc == 0 and C % Cs == 0 and KW <= NC
    row_spec = pl.BlockSpec((None, None, NC, Tc), lambda b, e, cc: (b, e, 0, 0))
    n_steps = B * E
    w_in_specs, w_out_specs, w_out_shapes, w_2d = [], [], [], []
    for w in weights:
        depth, rows, cols = w.shape[0], w.shape[1] * w.shape[2], w.shape[3]
        slab = rows // n_steps
        assert rows % n_steps == 0 and slab % 16 == 0, "bf16 rows travel in packs of 16"
        w_2d.append(w.reshape(depth * rows, cols))
        w_in_specs.append(pl.BlockSpec((slab, cols), lambda b, e, cc: (layer * n_steps + b * E + e, 0)))
        w_out_specs.append(pl.BlockSpec((slab, cols), lambda b, e, cc: (b * E + e, 0)))
        w_out_shapes.append(jax.ShapeDtypeStruct((rows, cols), BF16))
    outs = pl.pallas_call(
        functools.partial(_dispatch_kernel, C=C, Cs=Cs, Tc=Tc, NC=NC, KW=KW),
        grid_spec=pltpu.PrefetchScalarGridSpec(
            num_scalar_prefetch=1,
            grid=(B, E),
            in_specs=[row_spec, row_spec,
                      pl.BlockSpec((None, T, D), lambda b, e, cc: (b, 0, 0), pipeline_mode=pl.Buffered(1))]
                     + w_in_specs,
            out_specs=[pl.BlockSpec((None, None, C, D), lambda b, e, cc: (e, b, 0, 0)),
                       pl.BlockSpec((None, None, C, LANES), lambda b, e, cc: (e, b, 0, 0))] + w_out_specs,
            scratch_shapes=[pltpu.VMEM((Cs, D), F32), pltpu.VMEM((Cs, 1), F32)],
        ),
        out_shape=[jax.ShapeDtypeStruct((E, B, C, D), BF16), jax.ShapeDtypeStruct((E, B, C, LANES), F32)]
                  + w_out_shapes,
        compiler_params=_cparams(("arbitrary", "arbitrary")),
        name="dispatch",
    )(cc, pos.reshape(B, E, NC, Tc), aff_t.reshape(B, E, NC, Tc), h, *w_2d)
    w_bf16 = [o.reshape(w.shape[1:]) for o, w in zip(outs[2:], weights)]
    return outs[0], outs[1], w_bf16


def _ffn_kernel(x_ref, wg_ref, wu_ref, wd_ref, gate_ref, y_ref, acc_ref, *, FF, fc):
    x = x_ref[...]
    for kc in range(FF // fc):
        sl = slice(kc * fc, (kc + 1) * fc)
        g = jnp.dot(x, wg_ref[:, sl], preferred_element_type=F32)
        u = jnp.dot(x, wu_ref[:, sl], preferred_element_type=F32)
        a = (g * jax.nn.sigmoid(g) * u).astype(BF16)
        y = jnp.dot(a, wd_ref[sl, :], preferred_element_type=F32)
        if kc == 0:
            acc_ref[...] = y
        else:
            acc_ref[...] += y
    gate = gate_ref[...]
    for s in range(y_ref.shape[-1] // LANES):
        sl = slice(s * LANES, (s + 1) * LANES)
        y_ref[:, sl] = (acc_ref[:, sl] * gate).astype(y_ref.dtype)


def _ffn(xg, wg, wu, wd, gates, tm=1024, fc=256):
    E, R, D = xg.shape
    FF = wg.shape[-1]
    tm = min(tm, R)
    assert R % tm == 0 and FF % fc == 0
    return pl.pallas_call(
        functools.partial(_ffn_kernel, FF=FF, fc=fc),
        grid=(E, R // tm),
        in_specs=[pl.BlockSpec((None, tm, D), lambda e, i: (e, i, 0)),
                  pl.BlockSpec((None, D, FF), lambda e, i: (e, 0, 0)),
                  pl.BlockSpec((None, D, FF), lambda e, i: (e, 0, 0)),
                  pl.BlockSpec((None, FF, D), lambda e, i: (e, 0, 0)),
                  pl.BlockSpec((None, tm, LANES), lambda e, i: (e, i, 0))],
        out_specs=pl.BlockSpec((None, tm, D), lambda e, i: (e, i, 0)),
        out_shape=jax.ShapeDtypeStruct((E, R, D), BF16),
        scratch_shapes=[pltpu.VMEM((tm, D), F32)],
        compiler_params=_cparams(("parallel", "parallel")),
        name="ffn",
    )(xg, wg, wu, wd, gates)


def _combine_kernel(ss_ref, pos_ref, x1_ref, ye_ref, gfin_ref, out_ref, acc_ref, yall_ref, *, C, W, NT, final):
    E = ye_ref.shape[0]
    tm = x1_ref.shape[0]
    b = pl.program_id(0)
    i = pl.program_id(1)
    starts = []
    for e in range(E):
        row = (b * E + e) * (NT + 1) + i
        s0 = ss_ref[row]
        s1 = ss_ref[row + 1]
        a = pl.multiple_of(jnp.minimum((s0 // 16) * 16, C - W), 16)
        starts.append((a, jnp.maximum((s1 - a - 1) // W, 0)))
    lane = lax.broadcasted_iota(I32, (tm, 2 * W), 1)
    first = lane < W
    acc = x1_ref[...]
    for g in range(0, E, GROUP_E):
        pieces = []
        for e in range(g, g + GROUP_E, 2):
            for ee in (e, e + 1):
                yall_ref[ee * W:(ee + 1) * W, :] = ye_ref[ee, pl.ds(starts[ee][0], W), :]
            slot = jnp.where(first, starts[e][0] + lane, starts[e + 1][0] + lane - W)
            pcol = jnp.where(first, pos_ref[:, e:e + 1], pos_ref[:, e + 1:e + 2])
            pieces.append(jnp.where(pcol == slot, 1.0, 0.0).astype(BF16))
        onehot = jnp.concatenate(pieces, axis=1)
        acc = acc + jnp.dot(onehot, yall_ref[g * W:(g + GROUP_E) * W, :], preferred_element_type=F32)
    acc_ref[...] = acc
    n_extra_all = starts[0][1]
    for e in range(1, E):
        n_extra_all = n_extra_all + starts[e][1]

    @pl.when(n_extra_all > 0)
    def _():
        lane_w = lax.broadcasted_iota(I32, (tm, W), 1)
        for e in range(E):
            a0, n_extra = starts[e]
            pcol = pos_ref[:, e:e + 1]

            def chunk(k, carry, e=e, a0=a0, pcol=pcol):
                lo = a0 + k * W
                a = pl.multiple_of(jnp.minimum(lo, C - W), 16)
                slot = a + lane_w
                hit = jnp.logical_and(pcol == slot, slot >= lo)
                y = ye_ref[e, pl.ds(a, W), :]
                acc_ref[...] += jnp.dot(jnp.where(hit, 1.0, 0.0).astype(BF16), y, preferred_element_type=F32)
                return carry

            lax.fori_loop(1, n_extra + 1, chunk, 0)

    out = acc_ref[...]
    if final:
        out = _rms(out, gfin_ref[...])
    out_ref[...] = out


def _combine(ss, pos_t, x1, ye, gfin, C, final, tm=256, W=64):
    B, T, D = x1.shape
    E = ye.shape[0]
    tm = min(tm, T)
    NT = T // tm
    assert C % 16 == 0 and W % 16 == 0 and W <= C
    return pl.pallas_call(
        functools.partial(_combine_kernel, C=C, W=W, NT=NT, final=final),
        grid_spec=pltpu.PrefetchScalarGridSpec(
            num_scalar_prefetch=1,
            grid=(B, NT),
            in_specs=[pl.BlockSpec((None, tm, E), lambda b, i, ss: (b, i, 0)),
                      pl.BlockSpec((None, tm, D), lambda b, i, ss: (b, i, 0)),
                      pl.BlockSpec((E, None, C, D), lambda b, i, ss: (0, b, 0, 0), pipeline_mode=pl.Buffered(1)),
                      pl.BlockSpec((1, D), lambda b, i, ss: (0, 0))],
            out_specs=pl.BlockSpec((None, tm, D), lambda b, i, ss: (b, i, 0)),
            scratch_shapes=[pltpu.VMEM((tm, D), F32), pltpu.VMEM((E * W, D), BF16)],
        ),
        out_shape=jax.ShapeDtypeStruct((B, T, D), F32),
        compiler_params=_cparams(("arbitrary", "arbitrary")),
        name="combine",
    )(ss, pos_t, x1, ye, gfin.reshape(1, D))


def _rope_tables(T):
    pos = jnp.arange(T, dtype=F32)
    inv = ROPE_THETA ** (-jnp.arange(0, HEAD_DIM, 2, dtype=F32) / HEAD_DIM)
    ang = pos[:, None] * inv[None, :]
    cos = jnp.tile(jnp.cos(ang), (1, LANES // (HEAD_DIM // 2)))
    sin = jnp.sin(ang)
    sin = jnp.tile(jnp.concatenate([-sin, sin], axis=-1), (1, LANES // HEAD_DIM))
    return cos, sin


def _chunk_starts(pref, step, total):
    B, E, _ = pref.shape
    tail = jnp.full((B, E, 1), total, I32)
    return jnp.concatenate([pref[:, :, ::step], tail], axis=-1).reshape(-1)


def kernel(x, attn_norm, w_in, dil_out_norm, na_out_norm, na_rpb, w_out, ffn_norm, w_router,
           w_gate, w_up, w_down, final_norm):
    B, T, D = x.shape
    depth = w_in.shape[0]
    C = EC_CAPACITY * T // N_EXPERTS
    cos, sin = _rope_tables(T)
    tc_dispatch, tm_combine = LANES, min(256, T)
    for l in range(depth):
        qa, ka, va, qb, kb, vb = _qkv(x, attn_norm[l], w_in[l].astype(BF16), cos, sin)
        od = _dilated(qa, ka, va)
        on = _neighbourhood(qb, kb, vb, _na_bias_tables(na_rpb[l]))
        wr_t = w_router[l].T
        wr_hi = wr_t.astype(BF16)
        wr_lo = (wr_t - wr_hi.astype(F32)).astype(BF16)
        x1, h, aff_t = _post(od, on, x, w_out[l].astype(BF16), dil_out_norm[l], na_out_norm[l],
                             ffn_norm[l], wr_hi, wr_lo)
        pos, pref = _select(aff_t, C)
        xg, gates, (wg, wu, wd) = _dispatch(_chunk_starts(pref, tc_dispatch, C), pos, aff_t, h, C,
                                            (w_gate, w_up, w_down), l, Tc=tc_dispatch)
        ye = _ffn(xg.reshape(N_EXPERTS, B * C, D), wg, wu, wd, gates.reshape(N_EXPERTS, B * C, LANES))
        x = _combine(_chunk_starts(pref, tm_combine, C), pos.transpose(0, 2, 1), x1,
                     ye.reshape(N_EXPERTS, B, C, D), final_norm, C, final=(l == depth - 1), tm=tm_combine)
    return x
```

```python
import functools

import jax
import jax.numpy as jnp
from jax import lax
from jax.experimental import pallas as pl
from jax.experimental.pallas import tpu as pltpu

F32 = jnp.float32
BF16 = jnp.bfloat16
I32 = jnp.int32

HEAD_DIM = 64
LANES = 128
PAIRS = 4
GROUP = PAIRS * LANES
ATTN_SCALE = HEAD_DIM ** -0.5
ROPE_THETA = 10000.0
DIL_PATTERNS = ((128, 1), (512, 4), (2048, 16))
GRID_W = 64
NA_ROWS = 8
NA_COLS = 16
N_EXPERTS = 16
GROUP_E = 4
EC_CAPACITY = 2
EPS = 1e-6
NEG_INF = -1e30

VMEM_LIMIT = 56 * 1024 * 1024

_NT = (((1,), (1,)), ((), ()))


def _cparams(sem, vmem=VMEM_LIMIT):
    return pltpu.CompilerParams(dimension_semantics=sem, vmem_limit_bytes=vmem)


def _head0_mask():
    return lax.broadcasted_iota(I32, (1, LANES), 1) < HEAD_DIM


def _clip(x, lo, hi):
    return min(max(x, lo), hi) if isinstance(x, int) else jnp.clip(x, lo, hi)


def _aligned(x, m):
    return x if isinstance(x, int) else pl.multiple_of(x, m)


def _qkv_kernel(x_ref, g_ref, w_ref, cos_ref, sin_ref, *refs):
    n_dil = len(DIL_PATTERNS)
    dil_refs = [refs[t * n_dil:(t + 1) * n_dil] for t in range(3)]
    nbr_refs = refs[3 * n_dil:3 * n_dil + 3]
    ys_ref = refs[-1]
    tm = x_ref.shape[0]
    x = x_ref[...]
    ms = jnp.mean(x * x, axis=-1, keepdims=True)
    h = (x * lax.rsqrt(ms + EPS) * g_ref[...]).astype(BF16)
    cos = cos_ref[...]
    sin = sin_ref[...]
    lane = lax.broadcasted_iota(I32, (1, LANES), 1)
    first_half = (lane % HEAD_DIM) < HEAD_DIM // 2
    for gi in range(6):
        y = jnp.dot(h, w_ref[:, gi * GROUP:(gi + 1) * GROUP], preferred_element_type=F32)
        for p in range(PAIRS):
            s = y[:, p * LANES:(p + 1) * LANES]
            if gi < 2:
                partner = jnp.where(first_half, pltpu.roll(s, LANES - HEAD_DIM // 2, 1),
                                    pltpu.roll(s, HEAD_DIM // 2, 1))
                s = s * cos + partner * sin
            if gi in (0, 3):
                s = s * ATTN_SCALE
            if gi >= 3:
                nbr_refs[gi - 3][p] = s.astype(BF16)
                continue
            ys_ref[gi * PAIRS + p] = s
            for (_, d), o_ref in zip(DIL_PATTERNS, dil_refs[gi]):
                if d == 1:
                    o_ref[p, 0] = s.astype(BF16)
    for t in range(3):
        for (_, d), o_ref in zip(DIL_PATTERNS, dil_refs[t]):
            if d == 1:
                continue

            def regroup(p, carry, t=t, d=d, o_ref=o_ref):
                for r in range(d):
                    o_ref[p, r] = ys_ref[t * PAIRS + p, pl.ds(r, tm // d, stride=d), :].astype(BF16)
                return carry

            lax.fori_loop(0, PAIRS, regroup, 0, unroll=2)


def _qkv(x, g, w_bf16, cos, sin, tm=512):
    B, T, D = x.shape
    tm = min(tm, T)
    dils = [d for _, d in DIL_PATTERNS]
    assert all(tm % (16 * d) == 0 for d in dils), "bf16 rows travel in packs of 16"
    dil_shapes = [jax.ShapeDtypeStruct((B, PAIRS, d, T // d, LANES), BF16) for d in dils]
    dil_specs = [pl.BlockSpec((None, PAIRS, d, tm // d, LANES), lambda b, i: (b, 0, 0, i, 0)) for d in dils]
    nbr_shape = jax.ShapeDtypeStruct((B, PAIRS, T, LANES), BF16)
    nbr_spec = pl.BlockSpec((None, PAIRS, tm, LANES), lambda b, i: (b, 0, i, 0))
    outs = pl.pallas_call(
        _qkv_kernel,
        grid=(B, T // tm),
        in_specs=[
            pl.BlockSpec((None, tm, D), lambda b, i: (b, i, 0)),
            pl.BlockSpec((1, D), lambda b, i: (0, 0)),
            pl.BlockSpec((D, 6 * GROUP), lambda b, i: (0, 0)),
            pl.BlockSpec((tm, LANES), lambda b, i: (i, 0)),
            pl.BlockSpec((tm, LANES), lambda b, i: (i, 0)),
        ],
        out_specs=dil_specs * 3 + [nbr_spec] * 3,
        out_shape=dil_shapes * 3 + [nbr_shape] * 3,
        scratch_shapes=[pltpu.VMEM((3 * PAIRS, tm, LANES), F32)],
        compiler_params=_cparams(("parallel", "parallel")),
        name="qkv",
    )(x, g.reshape(1, D), w_bf16, cos, sin)
    n = len(dils)
    return (tuple(outs[:n]), tuple(outs[n:2 * n]), tuple(outs[2 * n:3 * n])) + tuple(outs[3 * n:])


def _dil_kernel(*refs, T, Lq, unroll):
    n_br = len(DIL_PATTERNS)
    q_refs, k_refs, v_refs = refs[:n_br], refs[n_br:2 * n_br], refs[2 * n_br:3 * n_br]
    o_ref, lse_ref, bias_ref = refs[3 * n_br:]
    head0 = _head0_mask()
    for bi, (window, d) in enumerate(DIL_PATTERNS):
        q_ref, k_ref, v_ref = q_refs[bi], k_refs[bi], v_refs[bi]
        half = window // (2 * d)
        L = T // d
        Wn = Lq + 2 * half
        nb = L // Lq
        rel0 = (lax.broadcasted_iota(I32, (Lq, Wn), 1) - lax.broadcasted_iota(I32, (Lq, Wn), 0))
        for ci, off in enumerate((0, -half, -2 * half)):
            bias_ref[ci] = jnp.where(jnp.abs(rel0 + off) <= half, 0.0, NEG_INF)

        def group(gi, carry, bi=bi, d=d, half=half, L=L, Wn=Wn, nb=nb):
            geo, q2s, ks, vs, biases = [], [], [], [], []
            for t in range(unroll):
                idx = gi * unroll + t
                r = idx // nb
                n = idx % nb
                m0 = n * Lq
                ws = jnp.clip(m0 - half, 0, L - Wn)

                q = q_ref[r, pl.ds(pl.multiple_of(m0, Lq), Lq), :]
                kvrows = pl.ds(pl.multiple_of(ws, half), Wn)
                ks.append(k_ref[r, kvrows, :])
                vs.append(v_ref[r, kvrows, :])
                biases.append(bias_ref[jnp.where(n == 0, 0, jnp.where(n == nb - 1, 2, 1))])
                zero = jnp.zeros_like(q)
                q2s.append(jnp.concatenate([jnp.where(head0, q, zero), jnp.where(head0, zero, q)], axis=0))
                geo.append(pl.ds(m0, Lq) if d == 1 else pl.ds(r + d * m0, Lq, stride=d))
            s = jnp.einsum('bqd,bkd->bqk', jnp.stack(q2s), jnp.stack(ks), preferred_element_type=F32)
            ps, stats = [], []
            for t in range(unroll):
                st = []
                for hs in (s[t, :Lq], s[t, Lq:]):
                    hs = hs + biases[t]
                    mh = jnp.max(hs, axis=-1, keepdims=True)
                    p = jnp.exp(hs - mh)
                    st.append((mh, jnp.sum(p, axis=-1, keepdims=True), p.astype(BF16)))
                ps.append(jnp.concatenate([st[0][2], st[1][2]], axis=0))
                stats.append(st)
            pv = jnp.einsum('bqk,bkd->bqd', jnp.stack(ps), jnp.stack(vs), preferred_element_type=F32)
            for t in range(unroll):
                st = stats[t]
                lb = jnp.where(head0, st[0][1], st[1][1])
                lse_b = jnp.where(head0, st[0][0], st[1][0]) + jnp.log(lb)
                ob = jnp.where(head0, pv[t, :Lq], pv[t, Lq:]) / lb
                qrows = geo[t]
                if bi == 0:
                    lse_ref[qrows, :] = lse_b
                    o_ref[qrows, :] = ob
                else:
                    lse_o = lse_ref[qrows, :]
                    mn = jnp.maximum(lse_o, lse_b)
                    wo = jnp.exp(lse_o - mn)
                    wb = jnp.exp(lse_b - mn)
                    den = wo + wb
                    o_ref[qrows, :] = (wo * o_ref[qrows, :] + wb * ob) / den
                    if bi < n_br - 1:
                        lse_ref[qrows, :] = mn + jnp.log(den)
            return carry

        lax.fori_loop(0, d * nb // unroll, group, 0)


def _dilated(qs, ks, vs, Lq=128, unroll=4):
    B, P = qs[0].shape[:2]
    T = qs[0].shape[2] * qs[0].shape[3]
    halves = {window // (2 * d) for window, d in DIL_PATTERNS}
    assert len(halves) == 1, "the band-mask scratch is sized for one half-width"
    half = halves.pop()
    assert half % 16 == 0, "bf16 rows travel in packs of 16"
    for window, d in DIL_PATTERNS:
        L = T // d
        assert T % d == 0 and L % Lq == 0 and L // Lq >= 2 and Lq >= half and Lq + 2 * half <= L, (T, window, d)
        assert (d * (L // Lq)) % unroll == 0
    in_specs = [pl.BlockSpec((None, None, d, T // d, LANES), lambda b, p: (b, p, 0, 0, 0))
                for _ in range(3) for _, d in DIL_PATTERNS]
    out_spec = pl.BlockSpec((None, None, T, LANES), lambda b, p: (b, p, 0, 0))
    return pl.pallas_call(
        functools.partial(_dil_kernel, T=T, Lq=Lq, unroll=unroll),
        grid=(B, P),
        in_specs=in_specs,
        out_specs=out_spec,
        out_shape=jax.ShapeDtypeStruct((B, P, T, LANES), F32),
        scratch_shapes=[pltpu.VMEM((T, LANES), F32), pltpu.VMEM((3, Lq, Lq + 2 * half), F32)],
        compiler_params=_cparams(("parallel", "parallel")),
        name="dilated",
    )(*qs, *ks, *vs)


def _na_bias_tables(rpb):
    H = rpb.shape[0]
    qc = jnp.arange(GRID_W)
    kc = jnp.arange(GRID_W)
    kr = jnp.arange(NA_ROWS)
    shift = jnp.arange(NA_ROWS)
    cs = jnp.clip(qc - NA_COLS // 2, 0, GRID_W - NA_COLS)
    valid = (kc[None, :] >= cs[:, None]) & (kc[None, :] < cs[:, None] + NA_COLS)
    dcol = jnp.clip(kc[None, :] - qc[:, None] + (NA_COLS - 1), 0, 2 * NA_COLS - 2)
    drow = kr[None, :] - shift[:, None] + (NA_ROWS - 1)
    row_pick = (drow[:, :, None] == jnp.arange(2 * NA_ROWS - 1)).astype(F32)
    col_pick = (dcol[:, :, None] == jnp.arange(2 * NA_COLS - 1)).astype(F32)
    hp = lax.Precision.HIGHEST
    t1 = jnp.einsum('hij,ski->hskj', rpb.astype(F32), row_pick, precision=hp)
    tbl = jnp.einsum('hskj,qcj->hsqkc', t1, col_pick, precision=hp)
    tbl = jnp.where(valid[None, None, :, None, :], tbl, NEG_INF)
    tbl = tbl.reshape(H // 2, 2, NA_ROWS, GRID_W, NA_ROWS * GRID_W).transpose(0, 2, 1, 3, 4)
    return tbl.reshape(H // 2, NA_ROWS, 2 * GRID_W, NA_ROWS * GRID_W)


def _na_kernel(q_ref, k_ref, v_ref, tbl_ref, o_ref, s_buf, p_buf, *, rows, rpu):
    head0 = _head0_mask()
    n_units = rows // rpu

    def geom(r):
        rs = _clip(r - NA_ROWS // 2, 0, rows - NA_ROWS)
        qrows = pl.ds(_aligned(r * GRID_W, GRID_W), GRID_W)
        kwin = pl.ds(_aligned(rs * GRID_W, GRID_W), NA_ROWS * GRID_W)
        return r - rs, qrows, kwin

    def stage_a(u):
        q2s, ks, biases = [], [], []
        for t in range(rpu):
            shift, qrows, kwin = geom(u * rpu + t)
            q = q_ref[qrows, :]
            zero = jnp.zeros_like(q)
            q2s.append(jnp.concatenate([jnp.where(head0, q, zero), jnp.where(head0, zero, q)], axis=0))
            ks.append(k_ref[kwin, :])
            biases.append(tbl_ref[shift])
        s = jnp.einsum('bqd,bkd->bqk', jnp.stack(q2s), jnp.stack(ks), preferred_element_type=F32)
        for t in range(rpu):
            s_buf[u % 2, t] = s[t] + biases[t]

    def stage_b(u):
        for t in range(rpu):
            s = s_buf[u % 2, t]
            p = jnp.exp(s - jnp.max(s, axis=-1, keepdims=True))
            p_buf[u % 2, t] = (p * (1.0 / jnp.sum(p, axis=-1, keepdims=True))).astype(BF16)

    def stage_c(u):
        geo = [geom(u * rpu + t) for t in range(rpu)]
        vs = jnp.stack([v_ref[kwin, :] for _, _, kwin in geo])
        o = jnp.einsum('bqk,bkd->bqd', p_buf[u % 2], vs, preferred_element_type=F32)
        for t in range(rpu):
            o_ref[geo[t][1], :] = jnp.where(head0, o[t, :GRID_W], o[t, GRID_W:])

    stage_a(0)
    stage_a(1)
    stage_b(0)

    def body(u, carry):
        stage_c(u)
        stage_b(u + 1)
        stage_a(u + 2)
        return carry

    lax.fori_loop(0, n_units - 2, body, 0, unroll=2)
    stage_c(n_units - 2)
    stage_b(n_units - 1)
    stage_c(n_units - 1)


def _neighbourhood(qb, kb, vb, tbl, rpu=2):
    B, P, T, _ = qb.shape
    rows = T // GRID_W
    assert T % GRID_W == 0 and rows >= NA_ROWS and rows % rpu == 0 and rows // rpu >= 3
    spec = pl.BlockSpec((None, None, T, LANES), lambda b, p: (b, p, 0, 0))
    return pl.pallas_call(
        functools.partial(_na_kernel, rows=rows, rpu=rpu),
        grid=(B, P),
        in_specs=[spec, spec, spec,
                  pl.BlockSpec((None, NA_ROWS, 2 * GRID_W, NA_ROWS * GRID_W), lambda b, p: (p, 0, 0, 0))],
        out_specs=spec,
        out_shape=jax.ShapeDtypeStruct((B, P, T, LANES), F32),
        scratch_shapes=[pltpu.VMEM((2, rpu, 2 * GRID_W, NA_ROWS * GRID_W), F32),
                        pltpu.VMEM((2, rpu, 2 * GRID_W, NA_ROWS * GRID_W), BF16)],
        compiler_params=_cparams(("parallel", "parallel")),
        name="nbr",
    )(qb, kb, vb, tbl)


def _rms(a, g):
    ms = jnp.mean(a * a, axis=-1, keepdims=True)
    return a * lax.rsqrt(ms + EPS) * g


def _post_kernel(od_ref, on_ref, x_ref, wo_ref, gd_ref, gn_ref, gf_ref, wrh_ref, wrl_ref,
                 x1_ref, h_ref, aff_ref):
    dil = jnp.concatenate([od_ref[p] for p in range(PAIRS)], axis=-1)
    nbr = jnp.concatenate([on_ref[p] for p in range(PAIRS)], axis=-1)
    mixed = jnp.concatenate([_rms(dil, gd_ref[...]), _rms(nbr, gn_ref[...])], axis=-1).astype(BF16)
    x1 = x_ref[...] + jnp.dot(mixed, wo_ref[...], preferred_element_type=F32)
    x1_ref[...] = x1
    h = _rms(x1, gf_ref[...])
    hh = h.astype(BF16)
    hl = (h - hh.astype(F32)).astype(BF16)
    h_ref[...] = hh
    wrh = wrh_ref[...]
    lt = (lax.dot_general(wrh, hh, _NT, preferred_element_type=F32)
          + lax.dot_general(wrh, hl, _NT, preferred_element_type=F32)
          + lax.dot_general(wrl_ref[...], hh, _NT, preferred_element_type=F32))
    m = jnp.max(lt, axis=0, keepdims=True)
    e = jnp.exp(lt - m)
    aff_ref[...] = e / jnp.sum(e, axis=0, keepdims=True)


def _post(od, on, x, wo_bf16, gd, gn, gf, wr_hi, wr_lo, tm=1024):
    B, T, D = x.shape
    tm = min(tm, T)
    E = wr_hi.shape[0]
    slab = pl.BlockSpec((None, PAIRS, tm, LANES), lambda b, i: (b, 0, i, 0))
    tok = pl.BlockSpec((None, tm, D), lambda b, i: (b, i, 0))
    full = lambda shape: pl.BlockSpec(shape, lambda b, i: (0,) * len(shape))
    return pl.pallas_call(
        _post_kernel,
        grid=(B, T // tm),
        in_specs=[slab, slab, tok, full((2 * GROUP, D)), full((1, GROUP)), full((1, GROUP)),
                  full((1, D)), full((E, D)), full((E, D))],
        out_specs=[tok, tok, pl.BlockSpec((None, E, tm), lambda b, i: (b, 0, i))],
        out_shape=[jax.ShapeDtypeStruct((B, T, D), F32), jax.ShapeDtypeStruct((B, T, D), BF16),
                   jax.ShapeDtypeStruct((B, E, T), F32)],
        compiler_params=_cparams(("parallel", "parallel")),
        name="post",
    )(od, on, x, wo_bf16, gd.reshape(1, GROUP), gn.reshape(1, GROUP), gf.reshape(1, D), wr_hi, wr_lo)


def _select_kernel(aff_ref, pos_ref, pref_ref, *, T, C):
    E = aff_ref.shape[0]
    bits = pltpu.bitcast(aff_ref[...], I32)
    t = jnp.zeros((E, 1), I32)
    for bit in range(30, -1, -1):
        cand = t | (1 << bit)
        cnt = jnp.sum(jnp.where(bits >= cand, 1.0, 0.0), axis=1, keepdims=True)
        t = jnp.where(cnt >= C, cand, t)
    gt = bits > t
    eq = bits == t
    need = C - jnp.sum(jnp.where(gt, 1.0, 0.0), axis=1, keepdims=True)
    tri = jnp.where(lax.broadcasted_iota(I32, (LANES, LANES), 0) < lax.broadcasted_iota(I32, (LANES, LANES), 1),
                    1.0, 0.0).astype(BF16)
    eq_carry = jnp.zeros((E, 1), F32)
    sel_carry = jnp.zeros((E, 1), F32)
    for c in range(T // LANES):
        sl = slice(c * LANES, (c + 1) * LANES)
        eqc = jnp.where(eq[:, sl], 1.0, 0.0)
        eq_rank = jnp.dot(eqc.astype(BF16), tri, preferred_element_type=F32) + eq_carry
        eq_carry = eq_carry + jnp.sum(eqc, axis=1, keepdims=True)
        sel = jnp.logical_or(gt[:, sl], jnp.logical_and(eq[:, sl], eq_rank < need))
        selc = jnp.where(sel, 1.0, 0.0)
        pref = (jnp.dot(selc.astype(BF16), tri, preferred_element_type=F32) + sel_carry).astype(I32)
        sel_carry = sel_carry + jnp.sum(selc, axis=1, keepdims=True)
        pref_ref[:, sl] = pref
        pos_ref[:, sl] = jnp.where(sel, pref, -1)


def _select(aff_t, C):
    B, E, T = aff_t.shape
    spec = pl.BlockSpec((None, E, T), lambda b: (b, 0, 0))
    return pl.pallas_call(
        functools.partial(_select_kernel, T=T, C=C),
        grid=(B,),
        in_specs=[spec],
        out_specs=[spec, spec],
        out_shape=[jax.ShapeDtypeStruct((B, E, T), I32), jax.ShapeDtypeStruct((B, E, T), I32)],
        compiler_params=_cparams(("parallel",)),
        name="select",
    )(aff_t)


def _dispatch_kernel(cc_ref, pos_ref, aff_ref, h_ref, wg_ref, wu_ref, wd_ref,
                     xg_ref, gate_ref, wgo_ref, wuo_ref, wdo_ref, *, C, G, Tc, NC, NS, Rb):
    wgo_ref[...] = wg_ref[...].astype(BF16)
    wuo_ref[...] = wu_ref[...].astype(BF16)
    wdo_ref[...] = wd_ref[...].astype(BF16)
    b, gi, sec = pl.program_id(0), pl.program_id(1), pl.program_id(2)
    n_grp = pl.num_programs(1)

    @pl.when(sec == 0)
    def _():
        xg_ref[...] = jnp.zeros_like(xg_ref)
        gate_ref[...] = jnp.zeros_like(gate_ref)

    band = lax.broadcasted_iota(I32, (Rb, Tc), 0)
    band_col = lax.broadcasted_iota(I32, (Rb, 1), 0)
    n_sec = NC // NS

    def geometry(c, e):
        row = ((b * n_grp + gi) * G + e) * (NC + 1) + c
        s0 = cc_ref[row]
        s1 = cc_ref[row + 1]
        r0 = pl.multiple_of(jnp.minimum((s0 // 16) * 16, C - Rb), 16)
        return s0, s1, r0, jnp.maximum((s1 - r0 - 1) // Rb, 0)

    def chunk(cs, n_extra_all):
        c = sec * n_sec + cs
        hc = h_ref[pl.ds(pl.multiple_of(c * Tc, Tc), Tc), :]
        geo, onehots, gsums = [], [], []
        for e in range(G):
            s0, s1, r0, n_extra = geometry(c, e)
            hit = pos_ref[e, pl.ds(c, 1), :] == r0 + band
            onehots.append(jnp.where(hit, 1.0, 0.0).astype(BF16))
            gsums.append(jnp.sum(jnp.where(hit, aff_ref[e, pl.ds(c, 1), :], 0.0), axis=1, keepdims=True))
            n_extra_all = n_extra_all + n_extra
            geo.append((s0, s1, r0))
        rows = jnp.dot(jnp.concatenate(onehots, axis=0), hc, preferred_element_type=F32)
        for e in range(G):
            s0, s1, r0 = geo[e]
            mine = jnp.logical_and(r0 + band_col >= s0, r0 + band_col < s1)
            dst = pl.ds(r0, Rb)
            xg_ref[e, dst, :] = jnp.where(mine, rows[e * Rb:(e + 1) * Rb].astype(BF16), xg_ref[e, dst, :])
            gate_ref[e, dst, :] = jnp.where(mine, gsums[e], gate_ref[e, dst, :])
        return n_extra_all

    n_extra_all = lax.fori_loop(0, n_sec, chunk, jnp.int32(0), unroll=2 if n_sec % 2 == 0 else 1)

    @pl.when(n_extra_all > 0)
    def _():
        for cs in range(n_sec):
            c = sec * n_sec + cs
            for e in range(G):
                s0, s1, r0, n_extra = geometry(c, e)

                def more(k, carry, c=c, e=e, s0=s0, s1=s1, r0=r0):
                    a = pl.multiple_of(jnp.minimum(r0 + k * Rb, C - Rb), 16)
                    hit = pos_ref[e, pl.ds(c, 1), :] == a + band
                    got = jnp.dot(jnp.where(hit, 1.0, 0.0).astype(BF16), h_ref[pl.ds(pl.multiple_of(c * Tc, Tc), Tc), :],
                                  preferred_element_type=F32)
                    gs = jnp.sum(jnp.where(hit, aff_ref[e, pl.ds(c, 1), :], 0.0), axis=1, keepdims=True)
                    mine = jnp.logical_and(a + band_col >= s0, a + band_col < s1)
                    dst = pl.ds(a, Rb)
                    xg_ref[e, dst, :] = jnp.where(mine, got.astype(BF16), xg_ref[e, dst, :])
                    gate_ref[e, dst, :] = jnp.where(mine, gs, gate_ref[e, dst, :])
                    return carry

                lax.fori_loop(1, n_extra + 1, more, 0)


def _dispatch(cc, pos, aff_t, h, C, weights, layer, Tc=256, G=4, NS=4, Rb=64):
    B, E, T = pos.shape
    D = h.shape[-1]
    NC = T // Tc
    NS = min(NS, NC)
    assert T % Tc == 0 and NC % NS == 0 and E % G == 0 and Rb % 16 == 0 and C % 16 == 0 and Rb <= C
    row_spec = pl.BlockSpec((None, G, NC, Tc), lambda b, g, s, cc: (b, g, 0, 0))
    n_grp = E // G
    n_steps = B * n_grp * NS
    step = lambda b, g, s: (b * n_grp + g) * NS + s
    w_in_specs, w_out_specs, w_out_shapes, w_2d = [], [], [], []
    for w in weights:
        depth, rows, cols = w.shape[0], w.shape[1] * w.shape[2], w.shape[3]
        slab = rows // n_steps
        assert rows % n_steps == 0 and slab % 16 == 0, "bf16 rows travel in packs of 16"
        w_2d.append(w.reshape(depth * rows, cols))
        w_in_specs.append(pl.BlockSpec((slab, cols), lambda b, g, s, cc: (layer * n_steps + step(b, g, s), 0)))
        w_out_specs.append(pl.BlockSpec((slab, cols), lambda b, g, s, cc: (step(b, g, s), 0)))
        w_out_shapes.append(jax.ShapeDtypeStruct((rows, cols), BF16))
    outs = pl.pallas_call(
        functools.partial(_dispatch_kernel, C=C, G=G, Tc=Tc, NC=NC, NS=NS, Rb=Rb),
        grid_spec=pltpu.PrefetchScalarGridSpec(
            num_scalar_prefetch=1,
            grid=(B, n_grp, NS),
            in_specs=[row_spec, row_spec,
                      pl.BlockSpec((None, T, D), lambda b, g, s, cc: (b, 0, 0), pipeline_mode=pl.Buffered(1))]
                     + w_in_specs,
            out_specs=[pl.BlockSpec((G, None, C, D), lambda b, g, s, cc: (g, b, 0, 0)),
                       pl.BlockSpec((G, None, C, LANES), lambda b, g, s, cc: (g, b, 0, 0))] + w_out_specs,
        ),
        out_shape=[jax.ShapeDtypeStruct((E, B, C, D), BF16), jax.ShapeDtypeStruct((E, B, C, LANES), F32)]
                  + w_out_shapes,
        compiler_params=_cparams(("arbitrary", "arbitrary", "arbitrary")),
        name="dispatch",
    )(cc, pos.reshape(B, E, NC, Tc), aff_t.reshape(B, E, NC, Tc), h, *w_2d)
    w_bf16 = [o.reshape(w.shape[1:]) for o, w in zip(outs[2:], weights)]
    return outs[0], outs[1], w_bf16


def _ffn_kernel(x_ref, wg_ref, wu_ref, wd_ref, gate_ref, y_ref, acc_ref, *, FF, fc):
    x = x_ref[...]
    for kc in range(FF // fc):
        sl = slice(kc * fc, (kc + 1) * fc)
        g = jnp.dot(x, wg_ref[:, sl], preferred_element_type=F32)
        u = jnp.dot(x, wu_ref[:, sl], preferred_element_type=F32)
        a = (g * jax.nn.sigmoid(g) * u).astype(BF16)
        y = jnp.dot(a, wd_ref[sl, :], preferred_element_type=F32)
        if kc == 0:
            acc_ref[...] = y
        else:
            acc_ref[...] += y
    gate = gate_ref[...]
    for s in range(y_ref.shape[-1] // LANES):
        sl = slice(s * LANES, (s + 1) * LANES)
        y_ref[:, sl] = (acc_ref[:, sl] * gate).astype(y_ref.dtype)


def _ffn(xg, wg, wu, wd, gates, tm=1024, fc=256):
    E, R, D = xg.shape
    FF = wg.shape[-1]
    tm = min(tm, R)
    assert R % tm == 0 and FF % fc == 0
    return pl.pallas_call(
        functools.partial(_ffn_kernel, FF=FF, fc=fc),
        grid=(E, R // tm),
        in_specs=[pl.BlockSpec((None, tm, D), lambda e, i: (e, i, 0)),
                  pl.BlockSpec((None, D, FF), lambda e, i: (e, 0, 0)),
                  pl.BlockSpec((None, D, FF), lambda e, i: (e, 0, 0)),
                  pl.BlockSpec((None, FF, D), lambda e, i: (e, 0, 0)),
                  pl.BlockSpec((None, tm, LANES), lambda e, i: (e, i, 0))],
        out_specs=pl.BlockSpec((None, tm, D), lambda e, i: (e, i, 0)),
        out_shape=jax.ShapeDtypeStruct((E, R, D), BF16),
        scratch_shapes=[pltpu.VMEM((tm, D), F32)],
        compiler_params=_cparams(("parallel", "parallel")),
        name="ffn",
    )(xg, wg, wu, wd, gates)


def _combine_kernel(ss_ref, pos_ref, x1_ref, ye_ref, gfin_ref, out_ref, acc_ref, yall_ref, *, C, W, NT, final):
    E = ye_ref.shape[0]
    tm = x1_ref.shape[0]
    b = pl.program_id(0)
    i = pl.program_id(1)
    starts = []
    for e in range(E):
        row = (b * E + e) * (NT + 1) + i
        s0 = ss_ref[row]
        s1 = ss_ref[row + 1]
        a = pl.multiple_of(jnp.minimum((s0 // 16) * 16, C - W), 16)
        starts.append((a, jnp.maximum((s1 - a - 1) // W, 0)))
    lane = lax.broadcasted_iota(I32, (tm, 2 * W), 1)
    first = lane < W
    acc = x1_ref[...]
    for g in range(0, E, GROUP_E):
        pieces = []
        for e in range(g, g + GROUP_E, 2):
            for ee in (e, e + 1):
                yall_ref[ee * W:(ee + 1) * W, :] = ye_ref[ee, pl.ds(starts[ee][0], W), :]
            slot = jnp.where(first, starts[e][0] + lane, starts[e + 1][0] + lane - W)
            pcol = jnp.where(first, pos_ref[:, e:e + 1], pos_ref[:, e + 1:e + 2])
            pieces.append(jnp.where(pcol == slot, 1.0, 0.0).astype(BF16))
        onehot = jnp.concatenate(pieces, axis=1)
        acc = acc + jnp.dot(onehot, yall_ref[g * W:(g + GROUP_E) * W, :], preferred_element_type=F32)
    acc_ref[...] = acc
    n_extra_all = starts[0][1]
    for e in range(1, E):
        n_extra_all = n_extra_all + starts[e][1]

    @pl.when(n_extra_all > 0)
    def _():
        lane_w = lax.broadcasted_iota(I32, (tm, W), 1)
        for e in range(E):
            a0, n_extra = starts[e]
            pcol = pos_ref[:, e:e + 1]

            def chunk(k, carry, e=e, a0=a0, pcol=pcol):
                lo = a0 + k * W
                a = pl.multiple_of(jnp.minimum(lo, C - W), 16)
                slot = a + lane_w
                hit = jnp.logical_and(pcol == slot, slot >= lo)
                y = ye_ref[e, pl.ds(a, W), :]
                acc_ref[...] += jnp.dot(jnp.where(hit, 1.0, 0.0).astype(BF16), y, preferred_element_type=F32)
                return carry

            lax.fori_loop(1, n_extra + 1, chunk, 0)

    out = acc_ref[...]
    if final:
        out = _rms(out, gfin_ref[...])
    out_ref[...] = out


def _combine(ss, pos_t, x1, ye, gfin, C, final, tm=256, W=64):
    B, T, D = x1.shape
    E = ye.shape[0]
    tm = min(tm, T)
    NT = T // tm
    assert C % 16 == 0 and W % 16 == 0 and W <= C
    return pl.pallas_call(
        functools.partial(_combine_kernel, C=C, W=W, NT=NT, final=final),
        grid_spec=pltpu.PrefetchScalarGridSpec(
            num_scalar_prefetch=1,
            grid=(B, NT),
            in_specs=[pl.BlockSpec((None, tm, E), lambda b, i, ss: (b, i, 0)),
                      pl.BlockSpec((None, tm, D), lambda b, i, ss: (b, i, 0)),
                      pl.BlockSpec((E, None, C, D), lambda b, i, ss: (0, b, 0, 0), pipeline_mode=pl.Buffered(1)),
                      pl.BlockSpec((1, D), lambda b, i, ss: (0, 0))],
            out_specs=pl.BlockSpec((None, tm, D), lambda b, i, ss: (b, i, 0)),
            scratch_shapes=[pltpu.VMEM((tm, D), F32), pltpu.VMEM((E * W, D), BF16)],
        ),
        out_shape=jax.ShapeDtypeStruct((B, T, D), F32),
        compiler_params=_cparams(("arbitrary", "arbitrary")),
        name="combine",
    )(ss, pos_t, x1, ye, gfin.reshape(1, D))


def _rope_tables(T):
    pos = jnp.arange(T, dtype=F32)
    inv = ROPE_THETA ** (-jnp.arange(0, HEAD_DIM, 2, dtype=F32) / HEAD_DIM)
    ang = pos[:, None] * inv[None, :]
    cos = jnp.tile(jnp.cos(ang), (1, LANES // (HEAD_DIM // 2)))
    sin = jnp.sin(ang)
    sin = jnp.tile(jnp.concatenate([-sin, sin], axis=-1), (1, LANES // HEAD_DIM))
    return cos, sin


def _chunk_starts(pref, step, total):
    B, E, _ = pref.shape
    tail = jnp.full((B, E, 1), total, I32)
    return jnp.concatenate([pref[:, :, ::step], tail], axis=-1).reshape(-1)


def kernel(x, attn_norm, w_in, dil_out_norm, na_out_norm, na_rpb, w_out, ffn_norm, w_router,
           w_gate, w_up, w_down, final_norm):
    B, T, D = x.shape
    depth = w_in.shape[0]
    C = EC_CAPACITY * T // N_EXPERTS
    cos, sin = _rope_tables(T)
    tc_dispatch, tm_combine = 256, min(256, T)
    for l in range(depth):
        qa, ka, va, qb, kb, vb = _qkv(x, attn_norm[l], w_in[l].astype(BF16), cos, sin)
        od = _dilated(qa, ka, va)
        on = _neighbourhood(qb, kb, vb, _na_bias_tables(na_rpb[l]))
        wr_t = w_router[l].T
        wr_hi = wr_t.astype(BF16)
        wr_lo = (wr_t - wr_hi.astype(F32)).astype(BF16)
        x1, h, aff_t = _post(od, on, x, w_out[l].astype(BF16), dil_out_norm[l], na_out_norm[l],
                             ffn_norm[l], wr_hi, wr_lo)
        pos, pref = _select(aff_t, C)
        xg, gates, (wg, wu, wd) = _dispatch(_chunk_starts(pref, tc_dispatch, C), pos, aff_t, h, C,
                                            (w_gate, w_up, w_down), l, Tc=tc_dispatch)
        ye = _ffn(xg.reshape(N_EXPERTS, B * C, D), wg, wu, wd, gates.reshape(N_EXPERTS, B * C, LANES))
        x = _combine(_chunk_starts(pref, tm_combine, C), pos.transpose(0, 2, 1), x1,
                     ye.reshape(N_EXPERTS, B, C, D), final_norm, C, final=(l == depth - 1), tm=tm_combine)
    return x
```

```python
import functools

import jax
import jax.numpy as jnp
from jax import lax
from jax.experimental import pallas as pl
from jax.experimental.pallas import tpu as pltpu

F32 = jnp.float32
BF16 = jnp.bfloat16
I32 = jnp.int32

HEAD_DIM = 64
LANES = 128
PAIRS = 4
GROUP = PAIRS * LANES
ATTN_SCALE = HEAD_DIM ** -0.5
ROPE_THETA = 10000.0
DIL_PATTERNS = ((128, 1), (512, 4), (2048, 16))
GRID_W = 64
NA_ROWS = 8
NA_COLS = 16
N_EXPERTS = 16
GROUP_E = 4
EC_CAPACITY = 2
EPS = 1e-6
NEG_INF = -1e30

VMEM_LIMIT = 56 * 1024 * 1024

_NT = (((1,), (1,)), ((), ()))


def _cparams(sem, vmem=VMEM_LIMIT):
    return pltpu.CompilerParams(dimension_semantics=sem, vmem_limit_bytes=vmem)


def _head0_mask():
    return lax.broadcasted_iota(I32, (1, LANES), 1) < HEAD_DIM


def _clip(x, lo, hi):
    return min(max(x, lo), hi) if isinstance(x, int) else jnp.clip(x, lo, hi)


def _aligned(x, m):
    return x if isinstance(x, int) else pl.multiple_of(x, m)


def _qkv_kernel(x_ref, g_ref, w_ref, cos_ref, sin_ref, *refs):
    n_dil = len(DIL_PATTERNS)
    dil_refs = [refs[t * n_dil:(t + 1) * n_dil] for t in range(3)]
    nbr_refs = refs[3 * n_dil:3 * n_dil + 3]
    ys_ref = refs[-1]
    tm = x_ref.shape[0]
    x = x_ref[...]
    ms = jnp.mean(x * x, axis=-1, keepdims=True)
    h = (x * lax.rsqrt(ms + EPS) * g_ref[...]).astype(BF16)
    cos = cos_ref[...]
    sin = sin_ref[...]
    lane = lax.broadcasted_iota(I32, (1, LANES), 1)
    first_half = (lane % HEAD_DIM) < HEAD_DIM // 2
    for gi in range(6):
        y = jnp.dot(h, w_ref[:, gi * GROUP:(gi + 1) * GROUP], preferred_element_type=F32)
        for p in range(PAIRS):
            s = y[:, p * LANES:(p + 1) * LANES]
            if gi < 2:
                partner = jnp.where(first_half, pltpu.roll(s, LANES - HEAD_DIM // 2, 1),
                                    pltpu.roll(s, HEAD_DIM // 2, 1))
                s = s * cos + partner * sin
            if gi in (0, 3):
                s = s * ATTN_SCALE
            if gi >= 3:
                nbr_refs[gi - 3][p] = s.astype(BF16)
                continue
            ys_ref[gi * PAIRS + p] = s
            for (_, d), o_ref in zip(DIL_PATTERNS, dil_refs[gi]):
                if d == 1:
                    o_ref[p, 0] = s.astype(BF16)
    for t in range(3):
        for (_, d), o_ref in zip(DIL_PATTERNS, dil_refs[t]):
            if d == 1:
                continue

            def regroup(p, carry, t=t, d=d, o_ref=o_ref):
                for r in range(d):
                    o_ref[p, r] = ys_ref[t * PAIRS + p, pl.ds(r, tm // d, stride=d), :].astype(BF16)
                return carry

            lax.fori_loop(0, PAIRS, regroup, 0, unroll=2)


def _qkv(x, g, w_bf16, cos, sin, tm=512):
    B, T, D = x.shape
    tm = min(tm, T)
    dils = [d for _, d in DIL_PATTERNS]
    assert all(tm % (16 * d) == 0 for d in dils), "bf16 rows travel in packs of 16"
    dil_shapes = [jax.ShapeDtypeStruct((B, PAIRS, d, T // d, LANES), BF16) for d in dils]
    dil_specs = [pl.BlockSpec((None, PAIRS, d, tm // d, LANES), lambda b, i: (b, 0, 0, i, 0)) for d in dils]
    nbr_shape = jax.ShapeDtypeStruct((B, PAIRS, T, LANES), BF16)
    nbr_spec = pl.BlockSpec((None, PAIRS, tm, LANES), lambda b, i: (b, 0, i, 0))
    outs = pl.pallas_call(
        _qkv_kernel,
        grid=(B, T // tm),
        in_specs=[
            pl.BlockSpec((None, tm, D), lambda b, i: (b, i, 0)),
            pl.BlockSpec((1, D), lambda b, i: (0, 0)),
            pl.BlockSpec((D, 6 * GROUP), lambda b, i: (0, 0)),
            pl.BlockSpec((tm, LANES), lambda b, i: (i, 0)),
            pl.BlockSpec((tm, LANES), lambda b, i: (i, 0)),
        ],
        out_specs=dil_specs * 3 + [nbr_spec] * 3,
        out_shape=dil_shapes * 3 + [nbr_shape] * 3,
        scratch_shapes=[pltpu.VMEM((3 * PAIRS, tm, LANES), F32)],
        compiler_params=_cparams(("parallel", "parallel")),
        name="qkv",
    )(x, g.reshape(1, D), w_bf16, cos, sin)
    n = len(dils)
    return (tuple(outs[:n]), tuple(outs[n:2 * n]), tuple(outs[2 * n:3 * n])) + tuple(outs[3 * n:])


def _dil_kernel(*refs, T, Lq, unroll):
    n_br = len(DIL_PATTERNS)
    q_refs, k_refs, v_refs = refs[:n_br], refs[n_br:2 * n_br], refs[2 * n_br:3 * n_br]
    o_ref, lse_ref, bias_ref = refs[3 * n_br:]
    head0 = _head0_mask()
    for bi, (window, d) in enumerate(DIL_PATTERNS):
        q_ref, k_ref, v_ref = q_refs[bi], k_refs[bi], v_refs[bi]
        half = window // (2 * d)
        L = T // d
        Wn = Lq + 2 * half
        nb = L // Lq
        rel0 = (lax.broadcasted_iota(I32, (Lq, Wn), 1) - lax.broadcasted_iota(I32, (Lq, Wn), 0))
        for ci, off in enumerate((0, -half, -2 * half)):
            bias_ref[ci] = jnp.where(jnp.abs(rel0 + off) <= half, 0.0, NEG_INF)

        def group(gi, carry, bi=bi, d=d, half=half, L=L, Wn=Wn, nb=nb):
            geo, q2s, ks, vs, biases = [], [], [], [], []
            for t in range(unroll):
                idx = gi * unroll + t
                r = idx // nb
                n = idx % nb
                m0 = n * Lq
                ws = jnp.clip(m0 - half, 0, L - Wn)

                q = q_ref[r, pl.ds(pl.multiple_of(m0, Lq), Lq), :]
                kvrows = pl.ds(pl.multiple_of(ws, half), Wn)
                ks.append(k_ref[r, kvrows, :])
                vs.append(v_ref[r, kvrows, :])
                biases.append(bias_ref[jnp.where(n == 0, 0, jnp.where(n == nb - 1, 2, 1))])
                zero = jnp.zeros_like(q)
                q2s.append(jnp.concatenate([jnp.where(head0, q, zero), jnp.where(head0, zero, q)], axis=0))
                geo.append(pl.ds(m0, Lq) if d == 1 else pl.ds(r + d * m0, Lq, stride=d))
            s = jnp.einsum('bqd,bkd->bqk', jnp.stack(q2s), jnp.stack(ks), preferred_element_type=F32)
            ps, stats = [], []
            for t in range(unroll):
                st = []
                for hs in (s[t, :Lq], s[t, Lq:]):
                    hs = hs + biases[t]
                    mh = jnp.max(hs, axis=-1, keepdims=True)
                    p = jnp.exp(hs - mh)
                    st.append((mh, jnp.sum(p, axis=-1, keepdims=True), p.astype(BF16)))
                ps.append(jnp.concatenate([st[0][2], st[1][2]], axis=0))
                stats.append(st)
            pv = jnp.einsum('bqk,bkd->bqd', jnp.stack(ps), jnp.stack(vs), preferred_element_type=F32)
            for t in range(unroll):
                st = stats[t]
                lb = jnp.where(head0, st[0][1], st[1][1])
                lse_b = jnp.where(head0, st[0][0], st[1][0]) + jnp.log(lb)
                ob = jnp.where(head0, pv[t, :Lq], pv[t, Lq:]) / lb
                qrows = geo[t]
                if bi == 0:
                    lse_ref[qrows, :] = lse_b
                    o_ref[qrows, :] = ob
                else:
                    lse_o = lse_ref[qrows, :]
                    mn = jnp.maximum(lse_o, lse_b)
                    wo = jnp.exp(lse_o - mn)
                    wb = jnp.exp(lse_b - mn)
                    den = wo + wb
                    o_ref[qrows, :] = (wo * o_ref[qrows, :] + wb * ob) / den
                    if bi < n_br - 1:
                        lse_ref[qrows, :] = mn + jnp.log(den)
            return carry

        lax.fori_loop(0, d * nb // unroll, group, 0)


def _dilated(qs, ks, vs, Lq=128, unroll=4):
    B, P = qs[0].shape[:2]
    T = qs[0].shape[2] * qs[0].shape[3]
    halves = {window // (2 * d) for window, d in DIL_PATTERNS}
    assert len(halves) == 1, "the band-mask scratch is sized for one half-width"
    half = halves.pop()
    assert half % 16 == 0, "bf16 rows travel in packs of 16"
    for window, d in DIL_PATTERNS:
        L = T // d
        assert T % d == 0 and L % Lq == 0 and L // Lq >= 2 and Lq >= half and Lq + 2 * half <= L, (T, window, d)
        assert (d * (L // Lq)) % unroll == 0
    in_specs = [pl.BlockSpec((None, None, d, T // d, LANES), lambda b, p: (b, p, 0, 0, 0))
                for _ in range(3) for _, d in DIL_PATTERNS]
    out_spec = pl.BlockSpec((None, None, T, LANES), lambda b, p: (b, p, 0, 0))
    return pl.pallas_call(
        functools.partial(_dil_kernel, T=T, Lq=Lq, unroll=unroll),
        grid=(B, P),
        in_specs=in_specs,
        out_specs=out_spec,
        out_shape=jax.ShapeDtypeStruct((B, P, T, LANES), F32),
        scratch_shapes=[pltpu.VMEM((T, LANES), F32), pltpu.VMEM((3, Lq, Lq + 2 * half), F32)],
        compiler_params=_cparams(("parallel", "parallel")),
        name="dilated",
    )(*qs, *ks, *vs)


def _na_bias_tables(rpb):
    H = rpb.shape[0]
    qc = jnp.arange(GRID_W)
    kc = jnp.arange(GRID_W)
    kr = jnp.arange(NA_ROWS)
    shift = jnp.arange(NA_ROWS)
    cs = jnp.clip(qc - NA_COLS // 2, 0, GRID_W - NA_COLS)
    valid = (kc[None, :] >= cs[:, None]) & (kc[None, :] < cs[:, None] + NA_COLS)
    dcol = jnp.clip(kc[None, :] - qc[:, None] + (NA_COLS - 1), 0, 2 * NA_COLS - 2)
    drow = kr[None, :] - shift[:, None] + (NA_ROWS - 1)
    row_pick = (drow[:, :, None] == jnp.arange(2 * NA_ROWS - 1)).astype(F32)
    col_pick = (dcol[:, :, None] == jnp.arange(2 * NA_COLS - 1)).astype(F32)
    hp = lax.Precision.HIGHEST
    t1 = jnp.einsum('hij,ski->hskj', rpb.astype(F32), row_pick, precision=hp)
    tbl = jnp.einsum('hskj,qcj->hsqkc', t1, col_pick, precision=hp)
    tbl = jnp.where(valid[None, None, :, None, :], tbl, NEG_INF)
    tbl = tbl.reshape(H // 2, 2, NA_ROWS, GRID_W, NA_ROWS * GRID_W).transpose(0, 2, 1, 3, 4)
    return tbl.reshape(H // 2, NA_ROWS, 2 * GRID_W, NA_ROWS * GRID_W)


def _na_kernel(q_ref, k_ref, v_ref, tbl_ref, o_ref, s_buf, p_buf, *, rows, rpu):
    head0 = _head0_mask()
    n_units = rows // rpu

    def geom(r):
        rs = _clip(r - NA_ROWS // 2, 0, rows - NA_ROWS)
        qrows = pl.ds(_aligned(r * GRID_W, GRID_W), GRID_W)
        kwin = pl.ds(_aligned(rs * GRID_W, GRID_W), NA_ROWS * GRID_W)
        return r - rs, qrows, kwin

    def stage_a(u):
        q2s, ks, biases = [], [], []
        for t in range(rpu):
            shift, qrows, kwin = geom(u * rpu + t)
            q = q_ref[qrows, :]
            zero = jnp.zeros_like(q)
            q2s.append(jnp.concatenate([jnp.where(head0, q, zero), jnp.where(head0, zero, q)], axis=0))
            ks.append(k_ref[kwin, :])
            biases.append(tbl_ref[shift])
        s = jnp.einsum('bqd,bkd->bqk', jnp.stack(q2s), jnp.stack(ks), preferred_element_type=F32)
        for t in range(rpu):
            s_buf[u % 2, t] = s[t] + biases[t]

    def stage_b(u):
        for t in range(rpu):
            s = s_buf[u % 2, t]
            p = jnp.exp(s - jnp.max(s, axis=-1, keepdims=True))
            p_buf[u % 2, t] = (p * (1.0 / jnp.sum(p, axis=-1, keepdims=True))).astype(BF16)

    def stage_c(u):
        geo = [geom(u * rpu + t) for t in range(rpu)]
        vs = jnp.stack([v_ref[kwin, :] for _, _, kwin in geo])
        o = jnp.einsum('bqk,bkd->bqd', p_buf[u % 2], vs, preferred_element_type=F32)
        for t in range(rpu):
            o_ref[geo[t][1], :] = jnp.where(head0, o[t, :GRID_W], o[t, GRID_W:])

    stage_a(0)
    stage_a(1)
    stage_b(0)

    def body(u, carry):
        stage_c(u)
        stage_b(u + 1)
        stage_a(u + 2)
        return carry

    lax.fori_loop(0, n_units - 2, body, 0, unroll=2)
    stage_c(n_units - 2)
    stage_b(n_units - 1)
    stage_c(n_units - 1)


def _neighbourhood(qb, kb, vb, tbl, rpu=2):
    B, P, T, _ = qb.shape
    rows = T // GRID_W
    assert T % GRID_W == 0 and rows >= NA_ROWS and rows % rpu == 0 and rows // rpu >= 3
    spec = pl.BlockSpec((None, None, T, LANES), lambda b, p: (b, p, 0, 0))
    return pl.pallas_call(
        functools.partial(_na_kernel, rows=rows, rpu=rpu),
        grid=(B, P),
        in_specs=[spec, spec, spec,
                  pl.BlockSpec((None, NA_ROWS, 2 * GRID_W, NA_ROWS * GRID_W), lambda b, p: (p, 0, 0, 0))],
        out_specs=spec,
        out_shape=jax.ShapeDtypeStruct((B, P, T, LANES), F32),
        scratch_shapes=[pltpu.VMEM((2, rpu, 2 * GRID_W, NA_ROWS * GRID_W), F32),
                        pltpu.VMEM((2, rpu, 2 * GRID_W, NA_ROWS * GRID_W), BF16)],
        compiler_params=_cparams(("parallel", "parallel")),
        name="nbr",
    )(qb, kb, vb, tbl)


def _rms(a, g):
    ms = jnp.mean(a * a, axis=-1, keepdims=True)
    return a * lax.rsqrt(ms + EPS) * g


def _post_kernel(od_ref, on_ref, x_ref, wo_ref, gd_ref, gn_ref, gf_ref, wrh_ref, wrl_ref,
                 x1_ref, h_ref, aff_ref):
    dil = jnp.concatenate([od_ref[p] for p in range(PAIRS)], axis=-1)
    nbr = jnp.concatenate([on_ref[p] for p in range(PAIRS)], axis=-1)
    mixed = jnp.concatenate([_rms(dil, gd_ref[...]), _rms(nbr, gn_ref[...])], axis=-1).astype(BF16)
    x1 = x_ref[...] + jnp.dot(mixed, wo_ref[...], preferred_element_type=F32)
    x1_ref[...] = x1
    h = _rms(x1, gf_ref[...])
    hh = h.astype(BF16)
    hl = (h - hh.astype(F32)).astype(BF16)
    h_ref[...] = hh
    wrh = wrh_ref[...]
    lt = (lax.dot_general(wrh, hh, _NT, preferred_element_type=F32)
          + lax.dot_general(wrh, hl, _NT, preferred_element_type=F32)
          + lax.dot_general(wrl_ref[...], hh, _NT, preferred_element_type=F32))
    m = jnp.max(lt, axis=0, keepdims=True)
    e = jnp.exp(lt - m)
    aff_ref[...] = e / jnp.sum(e, axis=0, keepdims=True)


def _post(od, on, x, wo_bf16, gd, gn, gf, wr_hi, wr_lo, tm=1024):
    B, T, D = x.shape
    tm = min(tm, T)
    E = wr_hi.shape[0]
    slab = pl.BlockSpec((None, PAIRS, tm, LANES), lambda b, i: (b, 0, i, 0))
    tok = pl.BlockSpec((None, tm, D), lambda b, i: (b, i, 0))
    full = lambda shape: pl.BlockSpec(shape, lambda b, i: (0,) * len(shape))
    return pl.pallas_call(
        _post_kernel,
        grid=(B, T // tm),
        in_specs=[slab, slab, tok, full((2 * GROUP, D)), full((1, GROUP)), full((1, GROUP)),
                  full((1, D)), full((E, D)), full((E, D))],
        out_specs=[tok, tok, pl.BlockSpec((None, E, tm), lambda b, i: (b, 0, i))],
        out_shape=[jax.ShapeDtypeStruct((B, T, D), F32), jax.ShapeDtypeStruct((B, T, D), BF16),
                   jax.ShapeDtypeStruct((B, E, T), F32)],
        compiler_params=_cparams(("parallel", "parallel")),
        name="post",
    )(od, on, x, wo_bf16, gd.reshape(1, GROUP), gn.reshape(1, GROUP), gf.reshape(1, D), wr_hi, wr_lo)


def _select_kernel(aff_ref, pos_ref, pref_ref, *, T, C):
    E = aff_ref.shape[0]
    aff = aff_ref[...]
    t = jnp.zeros((E, 1), I32)
    for bit in range(30, -1, -1):
        cand = t | (1 << bit)
        cnt = jnp.sum(jnp.where(aff >= pltpu.bitcast(cand, F32), 1.0, 0.0), axis=1, keepdims=True)
        t = jnp.where(cnt >= C, cand, t)
    thr = pltpu.bitcast(t, F32)
    gt = aff > thr
    eq = aff == thr
    need = C - jnp.sum(jnp.where(gt, 1.0, 0.0), axis=1, keepdims=True)
    tri = jnp.where(lax.broadcasted_iota(I32, (LANES, LANES), 0) < lax.broadcasted_iota(I32, (LANES, LANES), 1),
                    1.0, 0.0).astype(BF16)
    eq_carry = jnp.zeros((E, 1), F32)
    sel_carry = jnp.zeros((E, 1), F32)
    for c in range(T // LANES):
        sl = slice(c * LANES, (c + 1) * LANES)
        eqc = jnp.where(eq[:, sl], 1.0, 0.0)
        eq_rank = jnp.dot(eqc.astype(BF16), tri, preferred_element_type=F32) + eq_carry
        eq_carry = eq_carry + jnp.sum(eqc, axis=1, keepdims=True)
        sel = jnp.logical_or(gt[:, sl], jnp.logical_and(eq[:, sl], eq_rank < need))
        selc = jnp.where(sel, 1.0, 0.0)
        pref = (jnp.dot(selc.astype(BF16), tri, preferred_element_type=F32) + sel_carry).astype(I32)
        sel_carry = sel_carry + jnp.sum(selc, axis=1, keepdims=True)
        pref_ref[:, sl] = pref
        pos_ref[:, sl] = jnp.where(sel, pref, -1)


def _select(aff_t, C):
    B, E, T = aff_t.shape
    spec = pl.BlockSpec((None, E, T), lambda b: (b, 0, 0))
    return pl.pallas_call(
        functools.partial(_select_kernel, T=T, C=C),
        grid=(B,),
        in_specs=[spec],
        out_specs=[spec, spec],
        out_shape=[jax.ShapeDtypeStruct((B, E, T), I32), jax.ShapeDtypeStruct((B, E, T), I32)],
        compiler_params=_cparams(("parallel",)),
        name="select",
    )(aff_t)


def _dispatch_kernel(cc_ref, pos_ref, aff_ref, h_ref, wg_ref, wu_ref, wd_ref,
                     xg_ref, gate_ref, wgo_ref, wuo_ref, wdo_ref, *, C, G, Tc, NC, NS, Rb):
    wgo_ref[...] = wg_ref[...].astype(BF16)
    wuo_ref[...] = wu_ref[...].astype(BF16)
    wdo_ref[...] = wd_ref[...].astype(BF16)
    b, gi, sec = pl.program_id(0), pl.program_id(1), pl.program_id(2)
    n_grp = pl.num_programs(1)

    @pl.when(sec == 0)
    def _():
        xg_ref[...] = jnp.zeros_like(xg_ref)
        gate_ref[...] = jnp.zeros_like(gate_ref)

    band = lax.broadcasted_iota(I32, (Rb, Tc), 0)
    band_col = lax.broadcasted_iota(I32, (Rb, 1), 0)
    for cs in range(NC // NS):
        c = sec * (NC // NS) + cs
        hc = h_ref[pl.ds(pl.multiple_of(c * Tc, Tc), Tc), :]
        geo, onehots, gsums = [], [], []
        n_extra_all = jnp.int32(0)
        for e in range(G):
            row = ((b * n_grp + gi) * G + e) * (NC + 1) + c
            s0 = cc_ref[row]
            s1 = cc_ref[row + 1]
            r0 = pl.multiple_of(jnp.minimum((s0 // 16) * 16, C - Rb), 16)
            hit = pos_ref[e, pl.ds(c, 1), :] == r0 + band
            onehots.append(jnp.where(hit, 1.0, 0.0).astype(BF16))
            gsums.append(jnp.sum(jnp.where(hit, aff_ref[e, pl.ds(c, 1), :], 0.0), axis=1, keepdims=True))
            n_extra = jnp.maximum((s1 - r0 - 1) // Rb, 0)
            n_extra_all = n_extra_all + n_extra
            geo.append((s0, s1, r0, n_extra))
        rows = jnp.dot(jnp.concatenate(onehots, axis=0), hc, preferred_element_type=F32)
        for e in range(G):
            s0, s1, r0, _ = geo[e]
            mine = jnp.logical_and(r0 + band_col >= s0, r0 + band_col < s1)
            dst = pl.ds(r0, Rb)
            xg_ref[e, dst, :] = jnp.where(mine, rows[e * Rb:(e + 1) * Rb].astype(BF16), xg_ref[e, dst, :])
            gate_ref[e, dst, :] = jnp.where(mine, gsums[e], gate_ref[e, dst, :])

        @pl.when(n_extra_all > 0)
        def _(c=c, hc=hc, geo=geo):
            for e in range(G):
                s0, s1, r0, n_extra = geo[e]

                def more(k, carry, e=e, s0=s0, s1=s1, r0=r0):
                    a = pl.multiple_of(jnp.minimum(r0 + k * Rb, C - Rb), 16)
                    hit = pos_ref[e, pl.ds(c, 1), :] == a + band
                    got = jnp.dot(jnp.where(hit, 1.0, 0.0).astype(BF16), hc, preferred_element_type=F32)
                    gs = jnp.sum(jnp.where(hit, aff_ref[e, pl.ds(c, 1), :], 0.0), axis=1, keepdims=True)
                    mine = jnp.logical_and(a + band_col >= s0, a + band_col < s1)
                    dst = pl.ds(a, Rb)
                    xg_ref[e, dst, :] = jnp.where(mine, got.astype(BF16), xg_ref[e, dst, :])
                    gate_ref[e, dst, :] = jnp.where(mine, gs, gate_ref[e, dst, :])
                    return carry

                lax.fori_loop(1, n_extra + 1, more, 0)


def _dispatch(cc, pos, aff_t, h, C, weights, layer, Tc=256, G=4, NS=4, Rb=64):
    B, E, T = pos.shape
    D = h.shape[-1]
    NC = T // Tc
    NS = min(NS, NC)
    assert T % Tc == 0 and NC % NS == 0 and E % G == 0 and Rb % 16 == 0 and C % 16 == 0 and Rb <= C
    row_spec = pl.BlockSpec((None, G, NC, Tc), lambda b, g, s, cc: (b, g, 0, 0))
    n_grp = E // G
    n_steps = B * n_grp * NS
    step = lambda b, g, s: (b * n_grp + g) * NS + s
    w_in_specs, w_out_specs, w_out_shapes, w_2d = [], [], [], []
    for w in weights:
        depth, rows, cols = w.shape[0], w.shape[1] * w.shape[2], w.shape[3]
        slab = rows // n_steps
        assert rows % n_steps == 0 and slab % 16 == 0, "bf16 rows travel in packs of 16"
        w_2d.append(w.reshape(depth * rows, cols))
        w_in_specs.append(pl.BlockSpec((slab, cols), lambda b, g, s, cc: (layer * n_steps + step(b, g, s), 0)))
        w_out_specs.append(pl.BlockSpec((slab, cols), lambda b, g, s, cc: (step(b, g, s), 0)))
        w_out_shapes.append(jax.ShapeDtypeStruct((rows, cols), BF16))
    outs = pl.pallas_call(
        functools.partial(_dispatch_kernel, C=C, G=G, Tc=Tc, NC=NC, NS=NS, Rb=Rb),
        grid_spec=pltpu.PrefetchScalarGridSpec(
            num_scalar_prefetch=1,
            grid=(B, n_grp, NS),
            in_specs=[row_spec, row_spec,
                      pl.BlockSpec((None, T, D), lambda b, g, s, cc: (b, 0, 0), pipeline_mode=pl.Buffered(1))]
                     + w_in_specs,
            out_specs=[pl.BlockSpec((G, None, C, D), lambda b, g, s, cc: (g, b, 0, 0)),
                       pl.BlockSpec((G, None, C, LANES), lambda b, g, s, cc: (g, b, 0, 0))] + w_out_specs,
        ),
        out_shape=[jax.ShapeDtypeStruct((E, B, C, D), BF16), jax.ShapeDtypeStruct((E, B, C, LANES), F32)]
                  + w_out_shapes,
        compiler_params=_cparams(("arbitrary", "arbitrary", "arbitrary")),
        name="dispatch",
    )(cc, pos.reshape(B, E, NC, Tc), aff_t.reshape(B, E, NC, Tc), h, *w_2d)
    w_bf16 = [o.reshape(w.shape[1:]) for o, w in zip(outs[2:], weights)]
    return outs[0], outs[1], w_bf16


def _ffn_kernel(x_ref, wg_ref, wu_ref, wd_ref, gate_ref, y_ref, acc_ref, *, FF, fc):
    x = x_ref[...]
    for kc in range(FF // fc):
        sl = slice(kc * fc, (kc + 1) * fc)
        g = jnp.dot(x, wg_ref[:, sl], preferred_element_type=F32)
        u = jnp.dot(x, wu_ref[:, sl], preferred_element_type=F32)
        a = (g * jax.nn.sigmoid(g) * u).astype(BF16)
        y = jnp.dot(a, wd_ref[sl, :], preferred_element_type=F32)
        if kc == 0:
            acc_ref[...] = y
        else:
            acc_ref[...] += y
    gate = gate_ref[...]
    for s in range(y_ref.shape[-1] // LANES):
        sl = slice(s * LANES, (s + 1) * LANES)
        y_ref[:, sl] = (acc_ref[:, sl] * gate).astype(y_ref.dtype)


def _ffn(xg, wg, wu, wd, gates, tm=1024, fc=256):
    E, R, D = xg.shape
    FF = wg.shape[-1]
    tm = min(tm, R)
    assert R % tm == 0 and FF % fc == 0
    return pl.pallas_call(
        functools.partial(_ffn_kernel, FF=FF, fc=fc),
        grid=(E, R // tm),
        in_specs=[pl.BlockSpec((None, tm, D), lambda e, i: (e, i, 0)),
                  pl.BlockSpec((None, D, FF), lambda e, i: (e, 0, 0)),
                  pl.BlockSpec((None, D, FF), lambda e, i: (e, 0, 0)),
                  pl.BlockSpec((None, FF, D), lambda e, i: (e, 0, 0)),
                  pl.BlockSpec((None, tm, LANES), lambda e, i: (e, i, 0))],
        out_specs=pl.BlockSpec((None, tm, D), lambda e, i: (e, i, 0)),
        out_shape=jax.ShapeDtypeStruct((E, R, D), BF16),
        scratch_shapes=[pltpu.VMEM((tm, D), F32)],
        compiler_params=_cparams(("parallel", "parallel")),
        name="ffn",
    )(xg, wg, wu, wd, gates)


def _combine_kernel(ss_ref, pos_ref, x1_ref, ye_ref, gfin_ref, out_ref, acc_ref, yall_ref, *, C, W, NT, final):
    E = ye_ref.shape[0]
    tm = x1_ref.shape[0]
    b = pl.program_id(0)
    i = pl.program_id(1)
    starts = []
    for e in range(E):
        row = (b * E + e) * (NT + 1) + i
        s0 = ss_ref[row]
        s1 = ss_ref[row + 1]
        a = pl.multiple_of(jnp.minimum((s0 // 16) * 16, C - W), 16)
        starts.append((a, jnp.maximum((s1 - a - 1) // W, 0)))
    lane = lax.broadcasted_iota(I32, (tm, 2 * W), 1)
    first = lane < W
    acc = x1_ref[...]
    for g in range(0, E, GROUP_E):
        pieces = []
        for e in range(g, g + GROUP_E, 2):
            for ee in (e, e + 1):
                yall_ref[ee * W:(ee + 1) * W, :] = ye_ref[ee, pl.ds(starts[ee][0], W), :]
            slot = jnp.where(first, starts[e][0] + lane, starts[e + 1][0] + lane - W)
            pcol = jnp.where(first, pos_ref[:, e:e + 1], pos_ref[:, e + 1:e + 2])
            pieces.append(jnp.where(pcol == slot, 1.0, 0.0).astype(BF16))
        onehot = jnp.concatenate(pieces, axis=1)
        acc = acc + jnp.dot(onehot, yall_ref[g * W:(g + GROUP_E) * W, :], preferred_element_type=F32)
    acc_ref[...] = acc
    n_extra_all = starts[0][1]
    for e in range(1, E):
        n_extra_all = n_extra_all + starts[e][1]

    @pl.when(n_extra_all > 0)
    def _():
        lane_w = lax.broadcasted_iota(I32, (tm, W), 1)
        for e in range(E):
            a0, n_extra = starts[e]
            pcol = pos_ref[:, e:e + 1]

            def chunk(k, carry, e=e, a0=a0, pcol=pcol):
                lo = a0 + k * W
                a = pl.multiple_of(jnp.minimum(lo, C - W), 16)
                slot = a + lane_w
                hit = jnp.logical_and(pcol == slot, slot >= lo)
                y = ye_ref[e, pl.ds(a, W), :]
                acc_ref[...] += jnp.dot(jnp.where(hit, 1.0, 0.0).astype(BF16), y, preferred_element_type=F32)
                return carry

            lax.fori_loop(1, n_extra + 1, chunk, 0)

    out = acc_ref[...]
    if final:
        out = _rms(out, gfin_ref[...])
    out_ref[...] = out


def _combine(ss, pos_t, x1, ye, gfin, C, final, tm=256, W=64):
    B, T, D = x1.shape
    E = ye.shape[0]
    tm = min(tm, T)
    NT = T // tm
    assert C % 16 == 0 and W % 16 == 0 and W <= C
    return pl.pallas_call(
        functools.partial(_combine_kernel, C=C, W=W, NT=NT, final=final),
        grid_spec=pltpu.PrefetchScalarGridSpec(
            num_scalar_prefetch=1,
            grid=(B, NT),
            in_specs=[pl.BlockSpec((None, tm, E), lambda b, i, ss: (b, i, 0)),
                      pl.BlockSpec((None, tm, D), lambda b, i, ss: (b, i, 0)),
                      pl.BlockSpec((E, None, C, D), lambda b, i, ss: (0, b, 0, 0), pipeline_mode=pl.Buffered(1)),
                      pl.BlockSpec((1, D), lambda b, i, ss: (0, 0))],
            out_specs=pl.BlockSpec((None, tm, D), lambda b, i, ss: (b, i, 0)),
            scratch_shapes=[pltpu.VMEM((tm, D), F32), pltpu.VMEM((E * W, D), BF16)],
        ),
        out_shape=jax.ShapeDtypeStruct((B, T, D), F32),
        compiler_params=_cparams(("arbitrary", "arbitrary")),
        name="combine",
    )(ss, pos_t, x1, ye, gfin.reshape(1, D))


def _rope_tables(T):
    pos = jnp.arange(T, dtype=F32)
    inv = ROPE_THETA ** (-jnp.arange(0, HEAD_DIM, 2, dtype=F32) / HEAD_DIM)
    ang = pos[:, None] * inv[None, :]
    cos = jnp.tile(jnp.cos(ang), (1, LANES // (HEAD_DIM // 2)))
    sin = jnp.sin(ang)
    sin = jnp.tile(jnp.concatenate([-sin, sin], axis=-1), (1, LANES // HEAD_DIM))
    return cos, sin


def _chunk_starts(pref, step, total):
    B, E, _ = pref.shape
    tail = jnp.full((B, E, 1), total, I32)
    return jnp.concatenate([pref[:, :, ::step], tail], axis=-1).reshape(-1)


def kernel(x, attn_norm, w_in, dil_out_norm, na_out_norm, na_rpb, w_out, ffn_norm, w_router,
           w_gate, w_up, w_down, final_norm):
    B, T, D = x.shape
    depth = w_in.shape[0]
    C = EC_CAPACITY * T // N_EXPERTS
    cos, sin = _rope_tables(T)
    tc_dispatch, tm_combine = 256, min(256, T)
    for l in range(depth):
        qa, ka, va, qb, kb, vb = _qkv(x, attn_norm[l], w_in[l].astype(BF16), cos, sin)
        od = _dilated(qa, ka, va)
        on = _neighbourhood(qb, kb, vb, _na_bias_tables(na_rpb[l]))
        wr_t = w_router[l].T
        wr_hi = wr_t.astype(BF16)
        wr_lo = (wr_t - wr_hi.astype(F32)).astype(BF16)
        x1, h, aff_t = _post(od, on, x, w_out[l].astype(BF16), dil_out_norm[l], na_out_norm[l],
                             ffn_norm[l], wr_hi, wr_lo)
        pos, pref = _select(aff_t, C)
        xg, gates, (wg, wu, wd) = _dispatch(_chunk_starts(pref, tc_dispatch, C), pos, aff_t, h, C,
                                            (w_gate, w_up, w_down), l, Tc=tc_dispatch)
        ye = _ffn(xg.reshape(N_EXPERTS, B * C, D), wg, wu, wd, gates.reshape(N_EXPERTS, B * C, LANES))
        x = _combine(_chunk_starts(pref, tm_combine, C), pos.transpose(0, 2, 1), x1,
                     ye.reshape(N_EXPERTS, B, C, D), final_norm, C, final=(l == depth - 1), tm=tm_combine)
    return x
```

```python
import functools

import jax
import jax.numpy as jnp
from jax import lax
from jax.experimental import pallas as pl
from jax.experimental.pallas import tpu as pltpu

F32 = jnp.float32
BF16 = jnp.bfloat16
I32 = jnp.int32

HEAD_DIM = 64
LANES = 128
PAIRS = 4
GROUP = PAIRS * LANES
ATTN_SCALE = HEAD_DIM ** -0.5
ROPE_THETA = 10000.0
DIL_PATTERNS = ((128, 1), (512, 4), (2048, 16))
GRID_W = 64
NA_ROWS = 8
NA_COLS = 16
N_EXPERTS = 16
GROUP_E = 4
EC_CAPACITY = 2
EPS = 1e-6
NEG_INF = -1e30

VMEM_LIMIT = 56 * 1024 * 1024

_NT = (((1,), (1,)), ((), ()))


def _cparams(sem, vmem=VMEM_LIMIT):
    return pltpu.CompilerParams(dimension_semantics=sem, vmem_limit_bytes=vmem)


def _head0_mask():
    return lax.broadcasted_iota(I32, (1, LANES), 1) < HEAD_DIM


def _clip(x, lo, hi):
    return min(max(x, lo), hi) if isinstance(x, int) else jnp.clip(x, lo, hi)


def _aligned(x, m):
    return x if isinstance(x, int) else pl.multiple_of(x, m)


def _qkv_kernel(x_ref, g_ref, w_ref, cos_ref, sin_ref, *refs):
    n_dil = len(DIL_PATTERNS)
    dil_refs = [refs[t * n_dil:(t + 1) * n_dil] for t in range(3)]
    nbr_refs = refs[3 * n_dil:3 * n_dil + 3]
    ys_ref = refs[-1]
    tm = x_ref.shape[0]
    x = x_ref[...]
    ms = jnp.mean(x * x, axis=-1, keepdims=True)
    h = (x * lax.rsqrt(ms + EPS) * g_ref[...]).astype(BF16)
    cos = cos_ref[...]
    sin = sin_ref[...]
    lane = lax.broadcasted_iota(I32, (1, LANES), 1)
    first_half = (lane % HEAD_DIM) < HEAD_DIM // 2
    for gi in range(6):
        y = jnp.dot(h, w_ref[:, gi * GROUP:(gi + 1) * GROUP], preferred_element_type=F32)
        for p in range(PAIRS):
            s = y[:, p * LANES:(p + 1) * LANES]
            if gi < 2:
                partner = jnp.where(first_half, pltpu.roll(s, LANES - HEAD_DIM // 2, 1),
                                    pltpu.roll(s, HEAD_DIM // 2, 1))
                s = s * cos + partner * sin
            if gi in (0, 3):
                s = s * ATTN_SCALE
            if gi >= 3:
                nbr_refs[gi - 3][p] = s.astype(BF16)
                continue
            ys_ref[gi * PAIRS + p] = s
            for (_, d), o_ref in zip(DIL_PATTERNS, dil_refs[gi]):
                if d == 1:
                    o_ref[p, 0] = s.astype(BF16)
    for t in range(3):
        for (_, d), o_ref in zip(DIL_PATTERNS, dil_refs[t]):
            if d == 1:
                continue

            def regroup(p, carry, t=t, d=d, o_ref=o_ref):
                for r in range(d):
                    o_ref[p, r] = ys_ref[t * PAIRS + p, pl.ds(r, tm // d, stride=d), :].astype(BF16)
                return carry

            lax.fori_loop(0, PAIRS, regroup, 0, unroll=2)


def _qkv(x, g, w_bf16, cos, sin, tm=512):
    B, T, D = x.shape
    tm = min(tm, T)
    dils = [d for _, d in DIL_PATTERNS]
    assert all(tm % (16 * d) == 0 for d in dils), "bf16 rows travel in packs of 16"
    dil_shapes = [jax.ShapeDtypeStruct((B, PAIRS, d, T // d, LANES), BF16) for d in dils]
    dil_specs = [pl.BlockSpec((None, PAIRS, d, tm // d, LANES), lambda b, i: (b, 0, 0, i, 0)) for d in dils]
    nbr_shape = jax.ShapeDtypeStruct((B, PAIRS, T, LANES), BF16)
    nbr_spec = pl.BlockSpec((None, PAIRS, tm, LANES), lambda b, i: (b, 0, i, 0))
    outs = pl.pallas_call(
        _qkv_kernel,
        grid=(B, T // tm),
        in_specs=[
            pl.BlockSpec((None, tm, D), lambda b, i: (b, i, 0)),
            pl.BlockSpec((1, D), lambda b, i: (0, 0)),
            pl.BlockSpec((D, 6 * GROUP), lambda b, i: (0, 0)),
            pl.BlockSpec((tm, LANES), lambda b, i: (i, 0)),
            pl.BlockSpec((tm, LANES), lambda b, i: (i, 0)),
        ],
        out_specs=dil_specs * 3 + [nbr_spec] * 3,
        out_shape=dil_shapes * 3 + [nbr_shape] * 3,
        scratch_shapes=[pltpu.VMEM((3 * PAIRS, tm, LANES), F32)],
        compiler_params=_cparams(("parallel", "parallel")),
        name="qkv",
    )(x, g.reshape(1, D), w_bf16, cos, sin)
    n = len(dils)
    return (tuple(outs[:n]), tuple(outs[n:2 * n]), tuple(outs[2 * n:3 * n])) + tuple(outs[3 * n:])


def _dil_kernel(*refs, T, Lq, unroll):
    n_br = len(DIL_PATTERNS)
    q_refs, k_refs, v_refs = refs[:n_br], refs[n_br:2 * n_br], refs[2 * n_br:3 * n_br]
    o_ref, lse_ref, bias_ref = refs[3 * n_br:]
    head0 = _head0_mask()
    for bi, (window, d) in enumerate(DIL_PATTERNS):
        q_ref, k_ref, v_ref = q_refs[bi], k_refs[bi], v_refs[bi]
        half = window // (2 * d)
        L = T // d
        Wn = Lq + 2 * half
        nb = L // Lq
        rel0 = (lax.broadcasted_iota(I32, (Lq, Wn), 1) - lax.broadcasted_iota(I32, (Lq, Wn), 0))
        for ci, off in enumerate((0, -half, -2 * half)):
            bias_ref[ci] = jnp.where(jnp.abs(rel0 + off) <= half, 0.0, NEG_INF)

        def group(gi, carry, bi=bi, d=d, half=half, L=L, Wn=Wn, nb=nb):
            geo, q2s, ks, vs, biases = [], [], [], [], []
            for t in range(unroll):
                idx = gi * unroll + t
                r = idx // nb
                n = idx % nb
                m0 = n * Lq
                ws = jnp.clip(m0 - half, 0, L - Wn)

                q = q_ref[r, pl.ds(pl.multiple_of(m0, Lq), Lq), :]
                kvrows = pl.ds(pl.multiple_of(ws, half), Wn)
                ks.append(k_ref[r, kvrows, :])
                vs.append(v_ref[r, kvrows, :])
                biases.append(bias_ref[jnp.where(n == 0, 0, jnp.where(n == nb - 1, 2, 1))])
                zero = jnp.zeros_like(q)
                q2s.append(jnp.concatenate([jnp.where(head0, q, zero), jnp.where(head0, zero, q)], axis=0))
                geo.append(pl.ds(m0, Lq) if d == 1 else pl.ds(r + d * m0, Lq, stride=d))
            s = jnp.einsum('bqd,bkd->bqk', jnp.stack(q2s), jnp.stack(ks), preferred_element_type=F32)
            ps, stats = [], []
            for t in range(unroll):
                st = []
                for hs in (s[t, :Lq], s[t, Lq:]):
                    hs = hs + biases[t]
                    mh = jnp.max(hs, axis=-1, keepdims=True)
                    p = jnp.exp(hs - mh)
                    st.append((mh, jnp.sum(p, axis=-1, keepdims=True), p.astype(BF16)))
                ps.append(jnp.concatenate([st[0][2], st[1][2]], axis=0))
                stats.append(st)
            pv = jnp.einsum('bqk,bkd->bqd', jnp.stack(ps), jnp.stack(vs), preferred_element_type=F32)
            for t in range(unroll):
                st = stats[t]
                lb = jnp.where(head0, st[0][1], st[1][1])
                lse_b = jnp.where(head0, st[0][0], st[1][0]) + jnp.log(lb)
                ob = jnp.where(head0, pv[t, :Lq], pv[t, Lq:]) / lb
                qrows = geo[t]
                if bi == 0:
                    lse_ref[qrows, :] = lse_b
                    o_ref[qrows, :] = ob
                else:
                    lse_o = lse_ref[qrows, :]
                    mn = jnp.maximum(lse_o, lse_b)
                    wo = jnp.exp(lse_o - mn)
                    wb = jnp.exp(lse_b - mn)
                    den = wo + wb
                    o_ref[qrows, :] = (wo * o_ref[qrows, :] + wb * ob) / den
                    if bi < n_br - 1:
                        lse_ref[qrows, :] = mn + jnp.log(den)
            return carry

        lax.fori_loop(0, d * nb // unroll, group, 0, unroll=2)


def _dilated(qs, ks, vs, Lq=128, unroll=4):
    B, P = qs[0].shape[:2]
    T = qs[0].shape[2] * qs[0].shape[3]
    halves = {window // (2 * d) for window, d in DIL_PATTERNS}
    assert len(halves) == 1, "the band-mask scratch is sized for one half-width"
    half = halves.pop()
    assert half % 16 == 0, "bf16 rows travel in packs of 16"
    for window, d in DIL_PATTERNS:
        L = T // d
        assert T % d == 0 and L % Lq == 0 and L // Lq >= 2 and Lq >= half and Lq + 2 * half <= L, (T, window, d)
        assert (d * (L // Lq)) % unroll == 0
    in_specs = [pl.BlockSpec((None, None, d, T // d, LANES), lambda b, p: (b, p, 0, 0, 0))
                for _ in range(3) for _, d in DIL_PATTERNS]
    out_spec = pl.BlockSpec((None, None, T, LANES), lambda b, p: (b, p, 0, 0))
    return pl.pallas_call(
        functools.partial(_dil_kernel, T=T, Lq=Lq, unroll=unroll),
        grid=(B, P),
        in_specs=in_specs,
        out_specs=out_spec,
        out_shape=jax.ShapeDtypeStruct((B, P, T, LANES), F32),
        scratch_shapes=[pltpu.VMEM((T, LANES), F32), pltpu.VMEM((3, Lq, Lq + 2 * half), F32)],
        compiler_params=_cparams(("parallel", "parallel")),
        name="dilated",
    )(*qs, *ks, *vs)


def _na_bias_tables(rpb):
    H = rpb.shape[0]
    qc = jnp.arange(GRID_W)
    kc = jnp.arange(GRID_W)
    kr = jnp.arange(NA_ROWS)
    shift = jnp.arange(NA_ROWS)
    cs = jnp.clip(qc - NA_COLS // 2, 0, GRID_W - NA_COLS)
    valid = (kc[None, :] >= cs[:, None]) & (kc[None, :] < cs[:, None] + NA_COLS)
    dcol = jnp.clip(kc[None, :] - qc[:, None] + (NA_COLS - 1), 0, 2 * NA_COLS - 2)
    drow = kr[None, :] - shift[:, None] + (NA_ROWS - 1)
    row_pick = (drow[:, :, None] == jnp.arange(2 * NA_ROWS - 1)).astype(F32)
    col_pick = (dcol[:, :, None] == jnp.arange(2 * NA_COLS - 1)).astype(F32)
    hp = lax.Precision.HIGHEST
    t1 = jnp.einsum('hij,ski->hskj', rpb.astype(F32), row_pick, precision=hp)
    tbl = jnp.einsum('hskj,qcj->hsqkc', t1, col_pick, precision=hp)
    tbl = jnp.where(valid[None, None, :, None, :], tbl, NEG_INF)
    tbl = tbl.reshape(H // 2, 2, NA_ROWS, GRID_W, NA_ROWS * GRID_W).transpose(0, 2, 1, 3, 4)
    return tbl.reshape(H // 2, NA_ROWS, 2 * GRID_W, NA_ROWS * GRID_W)


def _na_kernel(q_ref, k_ref, v_ref, tbl_ref, o_ref, s_buf, p_buf, *, rows, rpu):
    head0 = _head0_mask()
    n_units = rows // rpu

    def geom(r):
        rs = _clip(r - NA_ROWS // 2, 0, rows - NA_ROWS)
        qrows = pl.ds(_aligned(r * GRID_W, GRID_W), GRID_W)
        kwin = pl.ds(_aligned(rs * GRID_W, GRID_W), NA_ROWS * GRID_W)
        return r - rs, qrows, kwin

    def stage_a(u):
        q2s, ks, biases = [], [], []
        for t in range(rpu):
            shift, qrows, kwin = geom(u * rpu + t)
            q = q_ref[qrows, :]
            zero = jnp.zeros_like(q)
            q2s.append(jnp.concatenate([jnp.where(head0, q, zero), jnp.where(head0, zero, q)], axis=0))
            ks.append(k_ref[kwin, :])
            biases.append(tbl_ref[shift])
        s = jnp.einsum('bqd,bkd->bqk', jnp.stack(q2s), jnp.stack(ks), preferred_element_type=F32)
        for t in range(rpu):
            s_buf[u % 2, t] = s[t] + biases[t]

    def stage_b(u):
        for t in range(rpu):
            s = s_buf[u % 2, t]
            p = jnp.exp(s - jnp.max(s, axis=-1, keepdims=True))
            p_buf[u % 2, t] = (p * (1.0 / jnp.sum(p, axis=-1, keepdims=True))).astype(BF16)

    def stage_c(u):
        geo = [geom(u * rpu + t) for t in range(rpu)]
        vs = jnp.stack([v_ref[kwin, :] for _, _, kwin in geo])
        o = jnp.einsum('bqk,bkd->bqd', p_buf[u % 2], vs, preferred_element_type=F32)
        for t in range(rpu):
            o_ref[geo[t][1], :] = jnp.where(head0, o[t, :GRID_W], o[t, GRID_W:])

    stage_a(0)
    stage_a(1)
    stage_b(0)

    def body(u, carry):
        stage_c(u)
        stage_b(u + 1)
        stage_a(u + 2)
        return carry

    lax.fori_loop(0, n_units - 2, body, 0, unroll=2)
    stage_c(n_units - 2)
    stage_b(n_units - 1)
    stage_c(n_units - 1)


def _neighbourhood(qb, kb, vb, tbl, rpu=2):
    B, P, T, _ = qb.shape
    rows = T // GRID_W
    assert T % GRID_W == 0 and rows >= NA_ROWS and rows % rpu == 0 and rows // rpu >= 3
    spec = pl.BlockSpec((None, None, T, LANES), lambda b, p: (b, p, 0, 0))
    return pl.pallas_call(
        functools.partial(_na_kernel, rows=rows, rpu=rpu),
        grid=(B, P),
        in_specs=[spec, spec, spec,
                  pl.BlockSpec((None, NA_ROWS, 2 * GRID_W, NA_ROWS * GRID_W), lambda b, p: (p, 0, 0, 0))],
        out_specs=spec,
        out_shape=jax.ShapeDtypeStruct((B, P, T, LANES), F32),
        scratch_shapes=[pltpu.VMEM((2, rpu, 2 * GRID_W, NA_ROWS * GRID_W), F32),
                        pltpu.VMEM((2, rpu, 2 * GRID_W, NA_ROWS * GRID_W), BF16)],
        compiler_params=_cparams(("parallel", "parallel")),
        name="nbr",
    )(qb, kb, vb, tbl)


def _rms(a, g):
    ms = jnp.mean(a * a, axis=-1, keepdims=True)
    return a * lax.rsqrt(ms + EPS) * g


def _post_kernel(od_ref, on_ref, x_ref, wo_ref, gd_ref, gn_ref, gf_ref, wrh_ref, wrl_ref,
                 x1_ref, h_ref, aff_ref):
    dil = jnp.concatenate([od_ref[p] for p in range(PAIRS)], axis=-1)
    nbr = jnp.concatenate([on_ref[p] for p in range(PAIRS)], axis=-1)
    mixed = jnp.concatenate([_rms(dil, gd_ref[...]), _rms(nbr, gn_ref[...])], axis=-1).astype(BF16)
    x1 = x_ref[...] + jnp.dot(mixed, wo_ref[...], preferred_element_type=F32)
    x1_ref[...] = x1
    h = _rms(x1, gf_ref[...])
    hh = h.astype(BF16)
    hl = (h - hh.astype(F32)).astype(BF16)
    h_ref[...] = hh
    wrh = wrh_ref[...]
    lt = (lax.dot_general(wrh, hh, _NT, preferred_element_type=F32)
          + lax.dot_general(wrh, hl, _NT, preferred_element_type=F32)
          + lax.dot_general(wrl_ref[...], hh, _NT, preferred_element_type=F32))
    m = jnp.max(lt, axis=0, keepdims=True)
    e = jnp.exp(lt - m)
    aff_ref[...] = e / jnp.sum(e, axis=0, keepdims=True)


def _post(od, on, x, wo_bf16, gd, gn, gf, wr_hi, wr_lo, tm=1024):
    B, T, D = x.shape
    tm = min(tm, T)
    E = wr_hi.shape[0]
    slab = pl.BlockSpec((None, PAIRS, tm, LANES), lambda b, i: (b, 0, i, 0))
    tok = pl.BlockSpec((None, tm, D), lambda b, i: (b, i, 0))
    full = lambda shape: pl.BlockSpec(shape, lambda b, i: (0,) * len(shape))
    return pl.pallas_call(
        _post_kernel,
        grid=(B, T // tm),
        in_specs=[slab, slab, tok, full((2 * GROUP, D)), full((1, GROUP)), full((1, GROUP)),
                  full((1, D)), full((E, D)), full((E, D))],
        out_specs=[tok, tok, pl.BlockSpec((None, E, tm), lambda b, i: (b, 0, i))],
        out_shape=[jax.ShapeDtypeStruct((B, T, D), F32), jax.ShapeDtypeStruct((B, T, D), BF16),
                   jax.ShapeDtypeStruct((B, E, T), F32)],
        compiler_params=_cparams(("parallel", "parallel")),
        name="post",
    )(od, on, x, wo_bf16, gd.reshape(1, GROUP), gn.reshape(1, GROUP), gf.reshape(1, D), wr_hi, wr_lo)


def _select_kernel(aff_ref, pos_ref, pref_ref, *, T, C):
    E = aff_ref.shape[0]
    aff = aff_ref[...]
    t = jnp.zeros((E, 1), I32)
    for bit in range(30, -1, -1):
        cand = t | (1 << bit)
        cnt = jnp.sum(jnp.where(aff >= pltpu.bitcast(cand, F32), 1.0, 0.0), axis=1, keepdims=True)
        t = jnp.where(cnt >= C, cand, t)
    thr = pltpu.bitcast(t, F32)
    gt = aff > thr
    eq = aff == thr
    need = C - jnp.sum(jnp.where(gt, 1.0, 0.0), axis=1, keepdims=True)
    tri = jnp.where(lax.broadcasted_iota(I32, (LANES, LANES), 0) < lax.broadcasted_iota(I32, (LANES, LANES), 1),
                    1.0, 0.0).astype(BF16)
    eq_carry = jnp.zeros((E, 1), F32)
    sel_carry = jnp.zeros((E, 1), F32)
    for c in range(T // LANES):
        sl = slice(c * LANES, (c + 1) * LANES)
        eqc = jnp.where(eq[:, sl], 1.0, 0.0)
        eq_rank = jnp.dot(eqc.astype(BF16), tri, preferred_element_type=F32) + eq_carry
        eq_carry = eq_carry + jnp.sum(eqc, axis=1, keepdims=True)
        sel = jnp.logical_or(gt[:, sl], jnp.logical_and(eq[:, sl], eq_rank < need))
        selc = jnp.where(sel, 1.0, 0.0)
        pref = (jnp.dot(selc.astype(BF16), tri, preferred_element_type=F32) + sel_carry).astype(I32)
        sel_carry = sel_carry + jnp.sum(selc, axis=1, keepdims=True)
        pref_ref[:, sl] = pref
        pos_ref[:, sl] = jnp.where(sel, pref, -1)


def _select(aff_t, C):
    B, E, T = aff_t.shape
    spec = pl.BlockSpec((None, E, T), lambda b: (b, 0, 0))
    return pl.pallas_call(
        functools.partial(_select_kernel, T=T, C=C),
        grid=(B,),
        in_specs=[spec],
        out_specs=[spec, spec],
        out_shape=[jax.ShapeDtypeStruct((B, E, T), I32), jax.ShapeDtypeStruct((B, E, T), I32)],
        compiler_params=_cparams(("parallel",)),
        name="select",
    )(aff_t)


def _dispatch_kernel(cc_ref, pos_ref, aff_ref, h_ref, wg_ref, wu_ref, wd_ref,
                     xg_ref, gate_ref, wgo_ref, wuo_ref, wdo_ref, *, C, G, Tc, NC, NS, Rb):
    wgo_ref[...] = wg_ref[...].astype(BF16)
    wuo_ref[...] = wu_ref[...].astype(BF16)
    wdo_ref[...] = wd_ref[...].astype(BF16)
    b, gi, sec = pl.program_id(0), pl.program_id(1), pl.program_id(2)
    n_grp = pl.num_programs(1)

    @pl.when(sec == 0)
    def _():
        xg_ref[...] = jnp.zeros_like(xg_ref)
        gate_ref[...] = jnp.zeros_like(gate_ref)

    band = lax.broadcasted_iota(I32, (Rb, Tc), 0)
    band_col = lax.broadcasted_iota(I32, (Rb, 1), 0)
    for cs in range(NC // NS):
        c = sec * (NC // NS) + cs
        hc = h_ref[pl.ds(pl.multiple_of(c * Tc, Tc), Tc), :]
        geo, onehots, gsums = [], [], []
        n_extra_all = jnp.int32(0)
        for e in range(G):
            row = ((b * n_grp + gi) * G + e) * (NC + 1) + c
            s0 = cc_ref[row]
            s1 = cc_ref[row + 1]
            r0 = pl.multiple_of(jnp.minimum((s0 // 16) * 16, C - Rb), 16)
            hit = pos_ref[e, pl.ds(c, 1), :] == r0 + band
            onehots.append(jnp.where(hit, 1.0, 0.0).astype(BF16))
            gsums.append(jnp.sum(jnp.where(hit, aff_ref[e, pl.ds(c, 1), :], 0.0), axis=1, keepdims=True))
            n_extra = jnp.maximum((s1 - r0 - 1) // Rb, 0)
            n_extra_all = n_extra_all + n_extra
            geo.append((s0, s1, r0, n_extra))
        rows = jnp.dot(jnp.concatenate(onehots, axis=0), hc, preferred_element_type=F32)
        for e in range(G):
            s0, s1, r0, _ = geo[e]
            mine = jnp.logical_and(r0 + band_col >= s0, r0 + band_col < s1)
            dst = pl.ds(r0, Rb)
            xg_ref[e, dst, :] = jnp.where(mine, rows[e * Rb:(e + 1) * Rb].astype(BF16), xg_ref[e, dst, :])
            gate_ref[e, dst, :] = jnp.where(mine, gsums[e], gate_ref[e, dst, :])

        @pl.when(n_extra_all > 0)
        def _(c=c, hc=hc, geo=geo):
            for e in range(G):
                s0, s1, r0, n_extra = geo[e]

                def more(k, carry, e=e, s0=s0, s1=s1, r0=r0):
                    a = pl.multiple_of(jnp.minimum(r0 + k * Rb, C - Rb), 16)
                    hit = pos_ref[e, pl.ds(c, 1), :] == a + band
                    got = jnp.dot(jnp.where(hit, 1.0, 0.0).astype(BF16), hc, preferred_element_type=F32)
                    gs = jnp.sum(jnp.where(hit, aff_ref[e, pl.ds(c, 1), :], 0.0), axis=1, keepdims=True)
                    mine = jnp.logical_and(a + band_col >= s0, a + band_col < s1)
                    dst = pl.ds(a, Rb)
                    xg_ref[e, dst, :] = jnp.where(mine, got.astype(BF16), xg_ref[e, dst, :])
                    gate_ref[e, dst, :] = jnp.where(mine, gs, gate_ref[e, dst, :])
                    return carry

                lax.fori_loop(1, n_extra + 1, more, 0)


def _dispatch(cc, pos, aff_t, h, C, weights, layer, Tc=256, G=4, NS=4, Rb=64):
    B, E, T = pos.shape
    D = h.shape[-1]
    NC = T // Tc
    NS = min(NS, NC)
    assert T % Tc == 0 and NC % NS == 0 and E % G == 0 and Rb % 16 == 0 and C % 16 == 0 and Rb <= C
    row_spec = pl.BlockSpec((None, G, NC, Tc), lambda b, g, s, cc: (b, g, 0, 0))
    n_grp = E // G
    n_steps = B * n_grp * NS
    step = lambda b, g, s: (b * n_grp + g) * NS + s
    w_in_specs, w_out_specs, w_out_shapes, w_2d = [], [], [], []
    for w in weights:
        depth, rows, cols = w.shape[0], w.shape[1] * w.shape[2], w.shape[3]
        slab = rows // n_steps
        assert rows % n_steps == 0 and slab % 16 == 0, "bf16 rows travel in packs of 16"
        w_2d.append(w.reshape(depth * rows, cols))
        w_in_specs.append(pl.BlockSpec((slab, cols), lambda b, g, s, cc: (layer * n_steps + step(b, g, s), 0)))
        w_out_specs.append(pl.BlockSpec((slab, cols), lambda b, g, s, cc: (step(b, g, s), 0)))
        w_out_shapes.append(jax.ShapeDtypeStruct((rows, cols), BF16))
    outs = pl.pallas_call(
        functools.partial(_dispatch_kernel, C=C, G=G, Tc=Tc, NC=NC, NS=NS, Rb=Rb),
        grid_spec=pltpu.PrefetchScalarGridSpec(
            num_scalar_prefetch=1,
            grid=(B, n_grp, NS),
            in_specs=[row_spec, row_spec,
                      pl.BlockSpec((None, T, D), lambda b, g, s, cc: (b, 0, 0), pipeline_mode=pl.Buffered(1))]
                     + w_in_specs,
            out_specs=[pl.BlockSpec((G, None, C, D), lambda b, g, s, cc: (g, b, 0, 0)),
                       pl.BlockSpec((G, None, C, LANES), lambda b, g, s, cc: (g, b, 0, 0))] + w_out_specs,
        ),
        out_shape=[jax.ShapeDtypeStruct((E, B, C, D), BF16), jax.ShapeDtypeStruct((E, B, C, LANES), F32)]
                  + w_out_shapes,
        compiler_params=_cparams(("arbitrary", "arbitrary", "arbitrary")),
        name="dispatch",
    )(cc, pos.reshape(B, E, NC, Tc), aff_t.reshape(B, E, NC, Tc), h, *w_2d)
    w_bf16 = [o.reshape(w.shape[1:]) for o, w in zip(outs[2:], weights)]
    return outs[0], outs[1], w_bf16


def _ffn_kernel(x_ref, wg_ref, wu_ref, wd_ref, gate_ref, y_ref, acc_ref, *, FF, fc):
    x = x_ref[...]
    for kc in range(FF // fc):
        sl = slice(kc * fc, (kc + 1) * fc)
        g = jnp.dot(x, wg_ref[:, sl], preferred_element_type=F32)
        u = jnp.dot(x, wu_ref[:, sl], preferred_element_type=F32)
        a = (g * jax.nn.sigmoid(g) * u).astype(BF16)
        y = jnp.dot(a, wd_ref[sl, :], preferred_element_type=F32)
        if kc == 0:
            acc_ref[...] = y
        else:
            acc_ref[...] += y
    gate = gate_ref[...]
    for s in range(y_ref.shape[-1] // LANES):
        sl = slice(s * LANES, (s + 1) * LANES)
        y_ref[:, sl] = (acc_ref[:, sl] * gate).astype(y_ref.dtype)


def _ffn(xg, wg, wu, wd, gates, tm=1024, fc=256):
    E, R, D = xg.shape
    FF = wg.shape[-1]
    tm = min(tm, R)
    assert R % tm == 0 and FF % fc == 0
    return pl.pallas_call(
        functools.partial(_ffn_kernel, FF=FF, fc=fc),
        grid=(E, R // tm),
        in_specs=[pl.BlockSpec((None, tm, D), lambda e, i: (e, i, 0)),
                  pl.BlockSpec((None, D, FF), lambda e, i: (e, 0, 0)),
                  pl.BlockSpec((None, D, FF), lambda e, i: (e, 0, 0)),
                  pl.BlockSpec((None, FF, D), lambda e, i: (e, 0, 0)),
                  pl.BlockSpec((None, tm, LANES), lambda e, i: (e, i, 0))],
        out_specs=pl.BlockSpec((None, tm, D), lambda e, i: (e, i, 0)),
        out_shape=jax.ShapeDtypeStruct((E, R, D), BF16),
        scratch_shapes=[pltpu.VMEM((tm, D), F32)],
        compiler_params=_cparams(("parallel", "parallel")),
        name="ffn",
    )(xg, wg, wu, wd, gates)


def _combine_kernel(ss_ref, pos_ref, x1_ref, ye_ref, gfin_ref, out_ref, acc_ref, yall_ref, *, C, W, NT, final):
    E = ye_ref.shape[0]
    tm = x1_ref.shape[0]
    b = pl.program_id(0)
    i = pl.program_id(1)
    starts = []
    for e in range(E):
        row = (b * E + e) * (NT + 1) + i
        s0 = ss_ref[row]
        s1 = ss_ref[row + 1]
        a = pl.multiple_of(jnp.minimum((s0 // 16) * 16, C - W), 16)
        starts.append((a, jnp.maximum((s1 - a - 1) // W, 0)))
    lane = lax.broadcasted_iota(I32, (tm, 2 * W), 1)
    first = lane < W
    acc = x1_ref[...]
    for g in range(0, E, GROUP_E):
        pieces = []
        for e in range(g, g + GROUP_E, 2):
            for ee in (e, e + 1):
                yall_ref[ee * W:(ee + 1) * W, :] = ye_ref[ee, pl.ds(starts[ee][0], W), :]
            slot = jnp.where(first, starts[e][0] + lane, starts[e + 1][0] + lane - W)
            pcol = jnp.where(first, pos_ref[:, e:e + 1], pos_ref[:, e + 1:e + 2])
            pieces.append(jnp.where(pcol == slot, 1.0, 0.0).astype(BF16))
        onehot = jnp.concatenate(pieces, axis=1)
        acc = acc + jnp.dot(onehot, yall_ref[g * W:(g + GROUP_E) * W, :], preferred_element_type=F32)
    acc_ref[...] = acc
    n_extra_all = starts[0][1]
    for e in range(1, E):
        n_extra_all = n_extra_all + starts[e][1]

    @pl.when(n_extra_all > 0)
    def _():
        lane_w = lax.broadcasted_iota(I32, (tm, W), 1)
        for e in range(E):
            a0, n_extra = starts[e]
            pcol = pos_ref[:, e:e + 1]

            def chunk(k, carry, e=e, a0=a0, pcol=pcol):
                lo = a0 + k * W
                a = pl.multiple_of(jnp.minimum(lo, C - W), 16)
                slot = a + lane_w
                hit = jnp.logical_and(pcol == slot, slot >= lo)
                y = ye_ref[e, pl.ds(a, W), :]
                acc_ref[...] += jnp.dot(jnp.where(hit, 1.0, 0.0).astype(BF16), y, preferred_element_type=F32)
                return carry

            lax.fori_loop(1, n_extra + 1, chunk, 0)

    out = acc_ref[...]
    if final:
        out = _rms(out, gfin_ref[...])
    out_ref[...] = out


def _combine(ss, pos_t, x1, ye, gfin, C, final, tm=256, W=64):
    B, T, D = x1.shape
    E = ye.shape[0]
    tm = min(tm, T)
    NT = T // tm
    assert C % 16 == 0 and W % 16 == 0 and W <= C
    return pl.pallas_call(
        functools.partial(_combine_kernel, C=C, W=W, NT=NT, final=final),
        grid_spec=pltpu.PrefetchScalarGridSpec(
            num_scalar_prefetch=1,
            grid=(B, NT),
            in_specs=[pl.BlockSpec((None, tm, E), lambda b, i, ss: (b, i, 0)),
                      pl.BlockSpec((None, tm, D), lambda b, i, ss: (b, i, 0)),
                      pl.BlockSpec((E, None, C, D), lambda b, i, ss: (0, b, 0, 0), pipeline_mode=pl.Buffered(1)),
                      pl.BlockSpec((1, D), lambda b, i, ss: (0, 0))],
            out_specs=pl.BlockSpec((None, tm, D), lambda b, i, ss: (b, i, 0)),
            scratch_shapes=[pltpu.VMEM((tm, D), F32), pltpu.VMEM((E * W, D), BF16)],
        ),
        out_shape=jax.ShapeDtypeStruct((B, T, D), F32),
        compiler_params=_cparams(("arbitrary", "arbitrary")),
        name="combine",
    )(ss, pos_t, x1, ye, gfin.reshape(1, D))


def _rope_tables(T):
    pos = jnp.arange(T, dtype=F32)
    inv = ROPE_THETA ** (-jnp.arange(0, HEAD_DIM, 2, dtype=F32) / HEAD_DIM)
    ang = pos[:, None] * inv[None, :]
    cos = jnp.tile(jnp.cos(ang), (1, LANES // (HEAD_DIM // 2)))
    sin = jnp.sin(ang)
    sin = jnp.tile(jnp.concatenate([-sin, sin], axis=-1), (1, LANES // HEAD_DIM))
    return cos, sin


def _chunk_starts(pref, step, total):
    B, E, _ = pref.shape
    tail = jnp.full((B, E, 1), total, I32)
    return jnp.concatenate([pref[:, :, ::step], tail], axis=-1).reshape(-1)


def kernel(x, attn_norm, w_in, dil_out_norm, na_out_norm, na_rpb, w_out, ffn_norm, w_router,
           w_gate, w_up, w_down, final_norm):
    B, T, D = x.shape
    depth = w_in.shape[0]
    C = EC_CAPACITY * T // N_EXPERTS
    cos, sin = _rope_tables(T)
    tc_dispatch, tm_combine = 256, min(256, T)
    for l in range(depth):
        qa, ka, va, qb, kb, vb = _qkv(x, attn_norm[l], w_in[l].astype(BF16), cos, sin)
        od = _dilated(qa, ka, va)
        on = _neighbourhood(qb, kb, vb, _na_bias_tables(na_rpb[l]))
        wr_t = w_router[l].T
        wr_hi = wr_t.astype(BF16)
        wr_lo = (wr_t - wr_hi.astype(F32)).astype(BF16)
        x1, h, aff_t = _post(od, on, x, w_out[l].astype(BF16), dil_out_norm[l], na_out_norm[l],
                             ffn_norm[l], wr_hi, wr_lo)
        pos, pref = _select(aff_t, C)
        xg, gates, (wg, wu, wd) = _dispatch(_chunk_starts(pref, tc_dispatch, C), pos, aff_t, h, C,
                                            (w_gate, w_up, w_down), l, Tc=tc_dispatch)
        ye = _ffn(xg.reshape(N_EXPERTS, B * C, D), wg, wu, wd, gates.reshape(N_EXPERTS, B * C, LANES))
        x = _combine(_chunk_starts(pref, tm_combine, C), pos.transpose(0, 2, 1), x1,
                     ye.reshape(N_EXPERTS, B, C, D), final_norm, C, final=(l == depth - 1), tm=tm_combine)
    return x
```

```python
import functools

import jax
import jax.numpy as jnp
from jax import lax
from jax.experimental import pallas as pl
from jax.experimental.pallas import tpu as pltpu

F32 = jnp.float32
BF16 = jnp.bfloat16
I32 = jnp.int32

HEAD_DIM = 64
LANES = 128
PAIRS = 4
GROUP = PAIRS * LANES
ATTN_SCALE = HEAD_DIM ** -0.5
ROPE_THETA = 10000.0
DIL_PATTERNS = ((128, 1), (512, 4), (2048, 16))
GRID_W = 64
NA_ROWS = 8
NA_COLS = 16
N_EXPERTS = 16
GROUP_E = 4
EC_CAPACITY = 2
EPS = 1e-6
NEG_INF = -1e30

VMEM_LIMIT = 56 * 1024 * 1024

_NT = (((1,), (1,)), ((), ()))


def _cparams(sem, vmem=VMEM_LIMIT):
    return pltpu.CompilerParams(dimension_semantics=sem, vmem_limit_bytes=vmem)


def _head0_mask():
    return lax.broadcasted_iota(I32, (1, LANES), 1) < HEAD_DIM


def _clip(x, lo, hi):
    return min(max(x, lo), hi) if isinstance(x, int) else jnp.clip(x, lo, hi)


def _aligned(x, m):
    return x if isinstance(x, int) else pl.multiple_of(x, m)


def _qkv_kernel(x_ref, g_ref, w_ref, cos_ref, sin_ref, *refs):
    n_dil = len(DIL_PATTERNS)
    dil_refs = [refs[t * n_dil:(t + 1) * n_dil] for t in range(3)]
    nbr_refs = refs[3 * n_dil:3 * n_dil + 3]
    ys_ref = refs[-1]
    tm = x_ref.shape[0]
    x = x_ref[...]
    ms = jnp.mean(x * x, axis=-1, keepdims=True)
    h = (x * lax.rsqrt(ms + EPS) * g_ref[...]).astype(BF16)
    cos = cos_ref[...]
    sin = sin_ref[...]
    lane = lax.broadcasted_iota(I32, (1, LANES), 1)
    first_half = (lane % HEAD_DIM) < HEAD_DIM // 2
    for gi in range(6):
        y = jnp.dot(h, w_ref[:, gi * GROUP:(gi + 1) * GROUP], preferred_element_type=F32)
        for p in range(PAIRS):
            s = y[:, p * LANES:(p + 1) * LANES]
            if gi < 2:
                partner = jnp.where(first_half, pltpu.roll(s, LANES - HEAD_DIM // 2, 1),
                                    pltpu.roll(s, HEAD_DIM // 2, 1))
                s = s * cos + partner * sin
            if gi in (0, 3):
                s = s * ATTN_SCALE
            if gi >= 3:
                nbr_refs[gi - 3][p] = s.astype(BF16)
                continue
            ys_ref[gi * PAIRS + p] = s
            for (_, d), o_ref in zip(DIL_PATTERNS, dil_refs[gi]):
                if d == 1:
                    o_ref[p, 0] = s.astype(BF16)
    for t in range(3):
        for (_, d), o_ref in zip(DIL_PATTERNS, dil_refs[t]):
            if d == 1:
                continue

            def regroup(p, carry, t=t, d=d, o_ref=o_ref):
                for r in range(d):
                    o_ref[p, r] = ys_ref[t * PAIRS + p, pl.ds(r, tm // d, stride=d), :].astype(BF16)
                return carry

            lax.fori_loop(0, PAIRS, regroup, 0, unroll=2)


def _qkv(x, g, w_bf16, cos, sin, tm=512):
    B, T, D = x.shape
    tm = min(tm, T)
    dils = [d for _, d in DIL_PATTERNS]
    assert all(tm % (16 * d) == 0 for d in dils), "bf16 rows travel in packs of 16"
    dil_shapes = [jax.ShapeDtypeStruct((B, PAIRS, d, T // d, LANES), BF16) for d in dils]
    dil_specs = [pl.BlockSpec((None, PAIRS, d, tm // d, LANES), lambda b, i: (b, 0, 0, i, 0)) for d in dils]
    nbr_shape = jax.ShapeDtypeStruct((B, PAIRS, T, LANES), BF16)
    nbr_spec = pl.BlockSpec((None, PAIRS, tm, LANES), lambda b, i: (b, 0, i, 0))
    outs = pl.pallas_call(
        _qkv_kernel,
        grid=(B, T // tm),
        in_specs=[
            pl.BlockSpec((None, tm, D), lambda b, i: (b, i, 0)),
            pl.BlockSpec((1, D), lambda b, i: (0, 0)),
            pl.BlockSpec((D, 6 * GROUP), lambda b, i: (0, 0)),
            pl.BlockSpec((tm, LANES), lambda b, i: (i, 0)),
            pl.BlockSpec((tm, LANES), lambda b, i: (i, 0)),
        ],
        out_specs=dil_specs * 3 + [nbr_spec] * 3,
        out_shape=dil_shapes * 3 + [nbr_shape] * 3,
        scratch_shapes=[pltpu.VMEM((3 * PAIRS, tm, LANES), F32)],
        compiler_params=_cparams(("parallel", "parallel")),
        name="qkv",
    )(x, g.reshape(1, D), w_bf16, cos, sin)
    n = len(dils)
    return (tuple(outs[:n]), tuple(outs[n:2 * n]), tuple(outs[2 * n:3 * n])) + tuple(outs[3 * n:])


def _dil_kernel(*refs, T, Lq, unroll):
    n_br = len(DIL_PATTERNS)
    q_refs, k_refs, v_refs = refs[:n_br], refs[n_br:2 * n_br], refs[2 * n_br:3 * n_br]
    o_ref, lse_ref, bias_ref = refs[3 * n_br:]
    head0 = _head0_mask()
    for bi, (window, d) in enumerate(DIL_PATTERNS):
        q_ref, k_ref, v_ref = q_refs[bi], k_refs[bi], v_refs[bi]
        half = window // (2 * d)
        L = T // d
        Wn = Lq + 2 * half
        nb = L // Lq
        rel0 = (lax.broadcasted_iota(I32, (Lq, Wn), 1) - lax.broadcasted_iota(I32, (Lq, Wn), 0))
        for ci, off in enumerate((0, -half, -2 * half)):
            bias_ref[ci] = jnp.where(jnp.abs(rel0 + off) <= half, 0.0, NEG_INF)

        def group(gi, carry, bi=bi, d=d, half=half, L=L, Wn=Wn, nb=nb):
            geo, q2s, ks, vs, biases = [], [], [], [], []
            for t in range(unroll):
                idx = gi * unroll + t
                r = idx // nb
                n = idx % nb
                m0 = n * Lq
                ws = jnp.clip(m0 - half, 0, L - Wn)

                q = q_ref[r, pl.ds(pl.multiple_of(m0, Lq), Lq), :]
                kvrows = pl.ds(pl.multiple_of(ws, half), Wn)
                ks.append(k_ref[r, kvrows, :])
                vs.append(v_ref[r, kvrows, :])
                biases.append(bias_ref[jnp.where(n == 0, 0, jnp.where(n == nb - 1, 2, 1))])
                zero = jnp.zeros_like(q)
                q2s.append(jnp.concatenate([jnp.where(head0, q, zero), jnp.where(head0, zero, q)], axis=0))
                geo.append(pl.ds(m0, Lq) if d == 1 else pl.ds(r + d * m0, Lq, stride=d))
            s = jnp.einsum('bqd,bkd->bqk', jnp.stack(q2s), jnp.stack(ks), preferred_element_type=F32)
            ps, stats = [], []
            for t in range(unroll):
                st = []
                for hs in (s[t, :Lq], s[t, Lq:]):
                    hs = hs + biases[t]
                    mh = jnp.max(hs, axis=-1, keepdims=True)
                    p = jnp.exp(hs - mh)
                    st.append((mh, jnp.sum(p, axis=-1, keepdims=True), p.astype(BF16)))
                ps.append(jnp.concatenate([st[0][2], st[1][2]], axis=0))
                stats.append(st)
            pv = jnp.einsum('bqk,bkd->bqd', jnp.stack(ps), jnp.stack(vs), preferred_element_type=F32)
            for t in range(unroll):
                st = stats[t]
                lb = jnp.where(head0, st[0][1], st[1][1])
                lse_b = jnp.where(head0, st[0][0], st[1][0]) + jnp.log(lb)
                ob = jnp.where(head0, pv[t, :Lq], pv[t, Lq:]) / lb
                qrows = geo[t]
                if bi == 0:
                    lse_ref[qrows, :] = lse_b
                    o_ref[qrows, :] = ob
                else:
                    lse_o = lse_ref[qrows, :]
                    mn = jnp.maximum(lse_o, lse_b)
                    wo = jnp.exp(lse_o - mn)
                    wb = jnp.exp(lse_b - mn)
                    den = wo + wb
                    o_ref[qrows, :] = (wo * o_ref[qrows, :] + wb * ob) / den
                    if bi < n_br - 1:
                        lse_ref[qrows, :] = mn + jnp.log(den)
            return carry

        lax.fori_loop(0, d * nb // unroll, group, 0, unroll=4)


def _dilated(qs, ks, vs, Lq=128, unroll=4):
    B, P = qs[0].shape[:2]
    T = qs[0].shape[2] * qs[0].shape[3]
    halves = {window // (2 * d) for window, d in DIL_PATTERNS}
    assert len(halves) == 1, "the band-mask scratch is sized for one half-width"
    half = halves.pop()
    assert half % 16 == 0, "bf16 rows travel in packs of 16"
    for window, d in DIL_PATTERNS:
        L = T // d
        assert T % d == 0 and L % Lq == 0 and L // Lq >= 2 and Lq >= half and Lq + 2 * half <= L, (T, window, d)
        assert (d * (L // Lq)) % unroll == 0
    in_specs = [pl.BlockSpec((None, None, d, T // d, LANES), lambda b, p: (b, p, 0, 0, 0))
                for _ in range(3) for _, d in DIL_PATTERNS]
    out_spec = pl.BlockSpec((None, None, T, LANES), lambda b, p: (b, p, 0, 0))
    return pl.pallas_call(
        functools.partial(_dil_kernel, T=T, Lq=Lq, unroll=unroll),
        grid=(B, P),
        in_specs=in_specs,
        out_specs=out_spec,
        out_shape=jax.ShapeDtypeStruct((B, P, T, LANES), F32),
        scratch_shapes=[pltpu.VMEM((T, LANES), F32), pltpu.VMEM((3, Lq, Lq + 2 * half), F32)],
        compiler_params=_cparams(("parallel", "parallel")),
        name="dilated",
    )(*qs, *ks, *vs)


def _na_bias_tables(rpb):
    H = rpb.shape[0]
    qc = jnp.arange(GRID_W)
    kc = jnp.arange(GRID_W)
    kr = jnp.arange(NA_ROWS)
    shift = jnp.arange(NA_ROWS)
    cs = jnp.clip(qc - NA_COLS // 2, 0, GRID_W - NA_COLS)
    valid = (kc[None, :] >= cs[:, None]) & (kc[None, :] < cs[:, None] + NA_COLS)
    dcol = jnp.clip(kc[None, :] - qc[:, None] + (NA_COLS - 1), 0, 2 * NA_COLS - 2)
    drow = kr[None, :] - shift[:, None] + (NA_ROWS - 1)
    row_pick = (drow[:, :, None] == jnp.arange(2 * NA_ROWS - 1)).astype(F32)
    col_pick = (dcol[:, :, None] == jnp.arange(2 * NA_COLS - 1)).astype(F32)
    hp = lax.Precision.HIGHEST
    t1 = jnp.einsum('hij,ski->hskj', rpb.astype(F32), row_pick, precision=hp)
    tbl = jnp.einsum('hskj,qcj->hsqkc', t1, col_pick, precision=hp)
    tbl = jnp.where(valid[None, None, :, None, :], tbl, NEG_INF)
    tbl = tbl.reshape(H // 2, 2, NA_ROWS, GRID_W, NA_ROWS * GRID_W).transpose(0, 2, 1, 3, 4)
    return tbl.reshape(H // 2, NA_ROWS, 2 * GRID_W, NA_ROWS * GRID_W)


def _na_kernel(q_ref, k_ref, v_ref, tbl_ref, o_ref, s_buf, p_buf, *, rows, rpu):
    head0 = _head0_mask()
    n_units = rows // rpu

    def geom(r):
        rs = _clip(r - NA_ROWS // 2, 0, rows - NA_ROWS)
        qrows = pl.ds(_aligned(r * GRID_W, GRID_W), GRID_W)
        kwin = pl.ds(_aligned(rs * GRID_W, GRID_W), NA_ROWS * GRID_W)
        return r - rs, qrows, kwin

    def stage_a(u):
        q2s, ks, biases = [], [], []
        for t in range(rpu):
            shift, qrows, kwin = geom(u * rpu + t)
            q = q_ref[qrows, :]
            zero = jnp.zeros_like(q)
            q2s.append(jnp.concatenate([jnp.where(head0, q, zero), jnp.where(head0, zero, q)], axis=0))
            ks.append(k_ref[kwin, :])
            biases.append(tbl_ref[shift])
        s = jnp.einsum('bqd,bkd->bqk', jnp.stack(q2s), jnp.stack(ks), preferred_element_type=F32)
        for t in range(rpu):
            s_buf[u % 2, t] = s[t] + biases[t]

    def stage_b(u):
        for t in range(rpu):
            s = s_buf[u % 2, t]
            p = jnp.exp(s - jnp.max(s, axis=-1, keepdims=True))
            p_buf[u % 2, t] = (p * (1.0 / jnp.sum(p, axis=-1, keepdims=True))).astype(BF16)

    def stage_c(u):
        geo = [geom(u * rpu + t) for t in range(rpu)]
        vs = jnp.stack([v_ref[kwin, :] for _, _, kwin in geo])
        o = jnp.einsum('bqk,bkd->bqd', p_buf[u % 2], vs, preferred_element_type=F32)
        for t in range(rpu):
            o_ref[geo[t][1], :] = jnp.where(head0, o[t, :GRID_W], o[t, GRID_W:])

    stage_a(0)
    stage_a(1)
    stage_b(0)

    def body(u, carry):
        stage_c(u)
        stage_b(u + 1)
        stage_a(u + 2)
        return carry

    lax.fori_loop(0, n_units - 2, body, 0, unroll=2)
    stage_c(n_units - 2)
    stage_b(n_units - 1)
    stage_c(n_units - 1)


def _neighbourhood(qb, kb, vb, tbl, rpu=2):
    B, P, T, _ = qb.shape
    rows = T // GRID_W
    assert T % GRID_W == 0 and rows >= NA_ROWS and rows % rpu == 0 and rows // rpu >= 3
    spec = pl.BlockSpec((None, None, T, LANES), lambda b, p: (b, p, 0, 0))
    return pl.pallas_call(
        functools.partial(_na_kernel, rows=rows, rpu=rpu),
        grid=(B, P),
        in_specs=[spec, spec, spec,
                  pl.BlockSpec((None, NA_ROWS, 2 * GRID_W, NA_ROWS * GRID_W), lambda b, p: (p, 0, 0, 0))],
        out_specs=spec,
        out_shape=jax.ShapeDtypeStruct((B, P, T, LANES), F32),
        scratch_shapes=[pltpu.VMEM((2, rpu, 2 * GRID_W, NA_ROWS * GRID_W), F32),
                        pltpu.VMEM((2, rpu, 2 * GRID_W, NA_ROWS * GRID_W), BF16)],
        compiler_params=_cparams(("parallel", "parallel")),
        name="nbr",
    )(qb, kb, vb, tbl)


def _rms(a, g):
    ms = jnp.mean(a * a, axis=-1, keepdims=True)
    return a * lax.rsqrt(ms + EPS) * g


def _post_kernel(od_ref, on_ref, x_ref, wo_ref, gd_ref, gn_ref, gf_ref, wrh_ref, wrl_ref,
                 x1_ref, h_ref, aff_ref):
    dil = jnp.concatenate([od_ref[p] for p in range(PAIRS)], axis=-1)
    nbr = jnp.concatenate([on_ref[p] for p in range(PAIRS)], axis=-1)
    mixed = jnp.concatenate([_rms(dil, gd_ref[...]), _rms(nbr, gn_ref[...])], axis=-1).astype(BF16)
    x1 = x_ref[...] + jnp.dot(mixed, wo_ref[...], preferred_element_type=F32)
    x1_ref[...] = x1
    h = _rms(x1, gf_ref[...])
    hh = h.astype(BF16)
    hl = (h - hh.astype(F32)).astype(BF16)
    h_ref[...] = hh
    wrh = wrh_ref[...]
    lt = (lax.dot_general(wrh, hh, _NT, preferred_element_type=F32)
          + lax.dot_general(wrh, hl, _NT, preferred_element_type=F32)
          + lax.dot_general(wrl_ref[...], hh, _NT, preferred_element_type=F32))
    m = jnp.max(lt, axis=0, keepdims=True)
    e = jnp.exp(lt - m)
    aff_ref[...] = e / jnp.sum(e, axis=0, keepdims=True)


def _post(od, on, x, wo_bf16, gd, gn, gf, wr_hi, wr_lo, tm=1024):
    B, T, D = x.shape
    tm = min(tm, T)
    E = wr_hi.shape[0]
    slab = pl.BlockSpec((None, PAIRS, tm, LANES), lambda b, i: (b, 0, i, 0))
    tok = pl.BlockSpec((None, tm, D), lambda b, i: (b, i, 0))
    full = lambda shape: pl.BlockSpec(shape, lambda b, i: (0,) * len(shape))
    return pl.pallas_call(
        _post_kernel,
        grid=(B, T // tm),
        in_specs=[slab, slab, tok, full((2 * GROUP, D)), full((1, GROUP)), full((1, GROUP)),
                  full((1, D)), full((E, D)), full((E, D))],
        out_specs=[tok, tok, pl.BlockSpec((None, E, tm), lambda b, i: (b, 0, i))],
        out_shape=[jax.ShapeDtypeStruct((B, T, D), F32), jax.ShapeDtypeStruct((B, T, D), BF16),
                   jax.ShapeDtypeStruct((B, E, T), F32)],
        compiler_params=_cparams(("parallel", "parallel")),
        name="post",
    )(od, on, x, wo_bf16, gd.reshape(1, GROUP), gn.reshape(1, GROUP), gf.reshape(1, D), wr_hi, wr_lo)


def _select_kernel(aff_ref, pos_ref, pref_ref, *, T, C):
    E = aff_ref.shape[0]
    aff = aff_ref[...]
    t = jnp.zeros((E, 1), I32)
    for bit in range(30, -1, -1):
        cand = t | (1 << bit)
        cnt = jnp.sum(jnp.where(aff >= pltpu.bitcast(cand, F32), 1.0, 0.0), axis=1, keepdims=True)
        t = jnp.where(cnt >= C, cand, t)
    thr = pltpu.bitcast(t, F32)
    gt = aff > thr
    eq = aff == thr
    need = C - jnp.sum(jnp.where(gt, 1.0, 0.0), axis=1, keepdims=True)
    tri = jnp.where(lax.broadcasted_iota(I32, (LANES, LANES), 0) < lax.broadcasted_iota(I32, (LANES, LANES), 1),
                    1.0, 0.0).astype(BF16)
    eq_carry = jnp.zeros((E, 1), F32)
    sel_carry = jnp.zeros((E, 1), F32)
    for c in range(T // LANES):
        sl = slice(c * LANES, (c + 1) * LANES)
        eqc = jnp.where(eq[:, sl], 1.0, 0.0)
        eq_rank = jnp.dot(eqc.astype(BF16), tri, preferred_element_type=F32) + eq_carry
        eq_carry = eq_carry + jnp.sum(eqc, axis=1, keepdims=True)
        sel = jnp.logical_or(gt[:, sl], jnp.logical_and(eq[:, sl], eq_rank < need))
        selc = jnp.where(sel, 1.0, 0.0)
        pref = (jnp.dot(selc.astype(BF16), tri, preferred_element_type=F32) + sel_carry).astype(I32)
        sel_carry = sel_carry + jnp.sum(selc, axis=1, keepdims=True)
        pref_ref[:, sl] = pref
        pos_ref[:, sl] = jnp.where(sel, pref, -1)


def _select(aff_t, C):
    B, E, T = aff_t.shape
    spec = pl.BlockSpec((None, E, T), lambda b: (b, 0, 0))
    return pl.pallas_call(
        functools.partial(_select_kernel, T=T, C=C),
        grid=(B,),
        in_specs=[spec],
        out_specs=[spec, spec],
        out_shape=[jax.ShapeDtypeStruct((B, E, T), I32), jax.ShapeDtypeStruct((B, E, T), I32)],
        compiler_params=_cparams(("parallel",)),
        name="select",
    )(aff_t)


def _dispatch_kernel(cc_ref, pos_ref, aff_ref, h_ref, wg_ref, wu_ref, wd_ref,
                     xg_ref, gate_ref, wgo_ref, wuo_ref, wdo_ref, *, C, G, Tc, NC, NS, Rb):
    wgo_ref[...] = wg_ref[...].astype(BF16)
    wuo_ref[...] = wu_ref[...].astype(BF16)
    wdo_ref[...] = wd_ref[...].astype(BF16)
    b, gi, sec = pl.program_id(0), pl.program_id(1), pl.program_id(2)
    n_grp = pl.num_programs(1)

    @pl.when(sec == 0)
    def _():
        xg_ref[...] = jnp.zeros_like(xg_ref)
        gate_ref[...] = jnp.zeros_like(gate_ref)

    band = lax.broadcasted_iota(I32, (Rb, Tc), 0)
    band_col = lax.broadcasted_iota(I32, (Rb, 1), 0)
    for cs in range(NC // NS):
        c = sec * (NC // NS) + cs
        hc = h_ref[pl.ds(pl.multiple_of(c * Tc, Tc), Tc), :]
        geo, onehots, gsums = [], [], []
        n_extra_all = jnp.int32(0)
        for e in range(G):
            row = ((b * n_grp + gi) * G + e) * (NC + 1) + c
            s0 = cc_ref[row]
            s1 = cc_ref[row + 1]
            r0 = pl.multiple_of(jnp.minimum((s0 // 16) * 16, C - Rb), 16)
            hit = pos_ref[e, pl.ds(c, 1), :] == r0 + band
            onehots.append(jnp.where(hit, 1.0, 0.0).astype(BF16))
            gsums.append(jnp.sum(jnp.where(hit, aff_ref[e, pl.ds(c, 1), :], 0.0), axis=1, keepdims=True))
            n_extra = jnp.maximum((s1 - r0 - 1) // Rb, 0)
            n_extra_all = n_extra_all + n_extra
            geo.append((s0, s1, r0, n_extra))
        rows = jnp.dot(jnp.concatenate(onehots, axis=0), hc, preferred_element_type=F32)
        for e in range(G):
            s0, s1, r0, _ = geo[e]
            mine = jnp.logical_and(r0 + band_col >= s0, r0 + band_col < s1)
            dst = pl.ds(r0, Rb)
            xg_ref[e, dst, :] = jnp.where(mine, rows[e * Rb:(e + 1) * Rb].astype(BF16), xg_ref[e, dst, :])
            gate_ref[e, dst, :] = jnp.where(mine, gsums[e], gate_ref[e, dst, :])

        @pl.when(n_extra_all > 0)
        def _(c=c, hc=hc, geo=geo):
            for e in range(G):
                s0, s1, r0, n_extra = geo[e]

                def more(k, carry, e=e, s0=s0, s1=s1, r0=r0):
                    a = pl.multiple_of(jnp.minimum(r0 + k * Rb, C - Rb), 16)
                    hit = pos_ref[e, pl.ds(c, 1), :] == a + band
                    got = jnp.dot(jnp.where(hit, 1.0, 0.0).astype(BF16), hc, preferred_element_type=F32)
                    gs = jnp.sum(jnp.where(hit, aff_ref[e, pl.ds(c, 1), :], 0.0), axis=1, keepdims=True)
                    mine = jnp.logical_and(a + band_col >= s0, a + band_col < s1)
                    dst = pl.ds(a, Rb)
                    xg_ref[e, dst, :] = jnp.where(mine, got.astype(BF16), xg_ref[e, dst, :])
                    gate_ref[e, dst, :] = jnp.where(mine, gs, gate_ref[e, dst, :])
                    return carry

                lax.fori_loop(1, n_extra + 1, more, 0)


def _dispatch(cc, pos, aff_t, h, C, weights, layer, Tc=256, G=4, NS=4, Rb=64):
    B, E, T = pos.shape
    D = h.shape[-1]
    NC = T // Tc
    NS = min(NS, NC)
    assert T % Tc == 0 and NC % NS == 0 and E % G == 0 and Rb % 16 == 0 and C % 16 == 0 and Rb <= C
    row_spec = pl.BlockSpec((None, G, NC, Tc), lambda b, g, s, cc: (b, g, 0, 0))
    n_grp = E // G
    n_steps = B * n_grp * NS
    step = lambda b, g, s: (b * n_grp + g) * NS + s
    w_in_specs, w_out_specs, w_out_shapes, w_2d = [], [], [], []
    for w in weights:
        depth, rows, cols = w.shape[0], w.shape[1] * w.shape[2], w.shape[3]
        slab = rows // n_steps
        assert rows % n_steps == 0 and slab % 16 == 0, "bf16 rows travel in packs of 16"
        w_2d.append(w.reshape(depth * rows, cols))
        w_in_specs.append(pl.BlockSpec((slab, cols), lambda b, g, s, cc: (layer * n_steps + step(b, g, s), 0)))
        w_out_specs.append(pl.BlockSpec((slab, cols), lambda b, g, s, cc: (step(b, g, s), 0)))
        w_out_shapes.append(jax.ShapeDtypeStruct((rows, cols), BF16))
    outs = pl.pallas_call(
        functools.partial(_dispatch_kernel, C=C, G=G, Tc=Tc, NC=NC, NS=NS, Rb=Rb),
        grid_spec=pltpu.PrefetchScalarGridSpec(
            num_scalar_prefetch=1,
            grid=(B, n_grp, NS),
            in_specs=[row_spec, row_spec,
                      pl.BlockSpec((None, T, D), lambda b, g, s, cc: (b, 0, 0), pipeline_mode=pl.Buffered(1))]
                     + w_in_specs,
            out_specs=[pl.BlockSpec((G, None, C, D), lambda b, g, s, cc: (g, b, 0, 0)),
                       pl.BlockSpec((G, None, C, LANES), lambda b, g, s, cc: (g, b, 0, 0))] + w_out_specs,
        ),
        out_shape=[jax.ShapeDtypeStruct((E, B, C, D), BF16), jax.ShapeDtypeStruct((E, B, C, LANES), F32)]
                  + w_out_shapes,
        compiler_params=_cparams(("arbitrary", "arbitrary", "arbitrary")),
        name="dispatch",
    )(cc, pos.reshape(B, E, NC, Tc), aff_t.reshape(B, E, NC, Tc), h, *w_2d)
    w_bf16 = [o.reshape(w.shape[1:]) for o, w in zip(outs[2:], weights)]
    return outs[0], outs[1], w_bf16


def _ffn_kernel(x_ref, wg_ref, wu_ref, wd_ref, gate_ref, y_ref, acc_ref, *, FF, fc):
    x = x_ref[...]
    for kc in range(FF // fc):
        sl = slice(kc * fc, (kc + 1) * fc)
        g = jnp.dot(x, wg_ref[:, sl], preferred_element_type=F32)
        u = jnp.dot(x, wu_ref[:, sl], preferred_element_type=F32)
        a = (g * jax.nn.sigmoid(g) * u).astype(BF16)
        y = jnp.dot(a, wd_ref[sl, :], preferred_element_type=F32)
        if kc == 0:
            acc_ref[...] = y
        else:
            acc_ref[...] += y
    gate = gate_ref[...]
    for s in range(y_ref.shape[-1] // LANES):
        sl = slice(s * LANES, (s + 1) * LANES)
        y_ref[:, sl] = (acc_ref[:, sl] * gate).astype(y_ref.dtype)


def _ffn(xg, wg, wu, wd, gates, tm=1024, fc=256):
    E, R, D = xg.shape
    FF = wg.shape[-1]
    tm = min(tm, R)
    assert R % tm == 0 and FF % fc == 0
    return pl.pallas_call(
        functools.partial(_ffn_kernel, FF=FF, fc=fc),
        grid=(E, R // tm),
        in_specs=[pl.BlockSpec((None, tm, D), lambda e, i: (e, i, 0)),
                  pl.BlockSpec((None, D, FF), lambda e, i: (e, 0, 0)),
                  pl.BlockSpec((None, D, FF), lambda e, i: (e, 0, 0)),
                  pl.BlockSpec((None, FF, D), lambda e, i: (e, 0, 0)),
                  pl.BlockSpec((None, tm, LANES), lambda e, i: (e, i, 0))],
        out_specs=pl.BlockSpec((None, tm, D), lambda e, i: (e, i, 0)),
        out_shape=jax.ShapeDtypeStruct((E, R, D), BF16),
        scratch_shapes=[pltpu.VMEM((tm, D), F32)],
        compiler_params=_cparams(("parallel", "parallel")),
        name="ffn",
    )(xg, wg, wu, wd, gates)


def _combine_kernel(ss_ref, pos_ref, x1_ref, ye_ref, gfin_ref, out_ref, acc_ref, yall_ref, *, C, W, NT, final):
    E = ye_ref.shape[0]
    tm = x1_ref.shape[0]
    b = pl.program_id(0)
    i = pl.program_id(1)
    starts = []
    for e in range(E):
        row = (b * E + e) * (NT + 1) + i
        s0 = ss_ref[row]
        s1 = ss_ref[row + 1]
        a = pl.multiple_of(jnp.minimum((s0 // 16) * 16, C - W), 16)
        starts.append((a, jnp.maximum((s1 - a - 1) // W, 0)))
    lane = lax.broadcasted_iota(I32, (tm, 2 * W), 1)
    first = lane < W
    acc = x1_ref[...]
    for g in range(0, E, GROUP_E):
        pieces = []
        for e in range(g, g + GROUP_E, 2):
            for ee in (e, e + 1):
                yall_ref[ee * W:(ee + 1) * W, :] = ye_ref[ee, pl.ds(starts[ee][0], W), :]
            slot = jnp.where(first, starts[e][0] + lane, starts[e + 1][0] + lane - W)
            pcol = jnp.where(first, pos_ref[:, e:e + 1], pos_ref[:, e + 1:e + 2])
            pieces.append(jnp.where(pcol == slot, 1.0, 0.0).astype(BF16))
        onehot = jnp.concatenate(pieces, axis=1)
        acc = acc + jnp.dot(onehot, yall_ref[g * W:(g + GROUP_E) * W, :], preferred_element_type=F32)
    acc_ref[...] = acc
    n_extra_all = starts[0][1]
    for e in range(1, E):
        n_extra_all = n_extra_all + starts[e][1]

    @pl.when(n_extra_all > 0)
    def _():
        lane_w = lax.broadcasted_iota(I32, (tm, W), 1)
        for e in range(E):
            a0, n_extra = starts[e]
            pcol = pos_ref[:, e:e + 1]

            def chunk(k, carry, e=e, a0=a0, pcol=pcol):
                lo = a0 + k * W
                a = pl.multiple_of(jnp.minimum(lo, C - W), 16)
                slot = a + lane_w
                hit = jnp.logical_and(pcol == slot, slot >= lo)
                y = ye_ref[e, pl.ds(a, W), :]
                acc_ref[...] += jnp.dot(jnp.where(hit, 1.0, 0.0).astype(BF16), y, preferred_element_type=F32)
                return carry

            lax.fori_loop(1, n_extra + 1, chunk, 0)

    out = acc_ref[...]
    if final:
        out = _rms(out, gfin_ref[...])
    out_ref[...] = out


def _combine(ss, pos_t, x1, ye, gfin, C, final, tm=256, W=64):
    B, T, D = x1.shape
    E = ye.shape[0]
    tm = min(tm, T)
    NT = T // tm
    assert C % 16 == 0 and W % 16 == 0 and W <= C
    return pl.pallas_call(
        functools.partial(_combine_kernel, C=C, W=W, NT=NT, final=final),
        grid_spec=pltpu.PrefetchScalarGridSpec(
            num_scalar_prefetch=1,
            grid=(B, NT),
            in_specs=[pl.BlockSpec((None, tm, E), lambda b, i, ss: (b, i, 0)),
                      pl.BlockSpec((None, tm, D), lambda b, i, ss: (b, i, 0)),
                      pl.BlockSpec((E, None, C, D), lambda b, i, ss: (0, b, 0, 0), pipeline_mode=pl.Buffered(1)),
                      pl.BlockSpec((1, D), lambda b, i, ss: (0, 0))],
            out_specs=pl.BlockSpec((None, tm, D), lambda b, i, ss: (b, i, 0)),
            scratch_shapes=[pltpu.VMEM((tm, D), F32), pltpu.VMEM((E * W, D), BF16)],
        ),
        out_shape=jax.ShapeDtypeStruct((B, T, D), F32),
        compiler_params=_cparams(("arbitrary", "arbitrary")),
        name="combine",
    )(ss, pos_t, x1, ye, gfin.reshape(1, D))


def _rope_tables(T):
    pos = jnp.arange(T, dtype=F32)
    inv = ROPE_THETA ** (-jnp.arange(0, HEAD_DIM, 2, dtype=F32) / HEAD_DIM)
    ang = pos[:, None] * inv[None, :]
    cos = jnp.tile(jnp.cos(ang), (1, LANES // (HEAD_DIM // 2)))
    sin = jnp.sin(ang)
    sin = jnp.tile(jnp.concatenate([-sin, sin], axis=-1), (1, LANES // HEAD_DIM))
    return cos, sin


def _chunk_starts(pref, step, total):
    B, E, _ = pref.shape
    tail = jnp.full((B, E, 1), total, I32)
    return jnp.concatenate([pref[:, :, ::step], tail], axis=-1).reshape(-1)


def kernel(x, attn_norm, w_in, dil_out_norm, na_out_norm, na_rpb, w_out, ffn_norm, w_router,
           w_gate, w_up, w_down, final_norm):
    B, T, D = x.shape
    depth = w_in.shape[0]
    C = EC_CAPACITY * T // N_EXPERTS
    cos, sin = _rope_tables(T)
    tc_dispatch, tm_combine = 256, min(256, T)
    for l in range(depth):
        qa, ka, va, qb, kb, vb = _qkv(x, attn_norm[l], w_in[l].astype(BF16), cos, sin)
        od = _dilated(qa, ka, va)
        on = _neighbourhood(qb, kb, vb, _na_bias_tables(na_rpb[l]))
        wr_t = w_router[l].T
        wr_hi = wr_t.astype(BF16)
        wr_lo = (wr_t - wr_hi.astype(F32)).astype(BF16)
        x1, h, aff_t = _post(od, on, x, w_out[l].astype(BF16), dil_out_norm[l], na_out_norm[l],
                             ffn_norm[l], wr_hi, wr_lo)
        pos, pref = _select(aff_t, C)
        xg, gates, (wg, wu, wd) = _dispatch(_chunk_starts(pref, tc_dispatch, C), pos, aff_t, h, C,
                                            (w_gate, w_up, w_down), l, Tc=tc_dispatch)
        ye = _ffn(xg.reshape(N_EXPERTS, B * C, D), wg, wu, wd, gates.reshape(N_EXPERTS, B * C, LANES))
        x = _combine(_chunk_starts(pref, tm_combine, C), pos.transpose(0, 2, 1), x1,
                     ye.reshape(N_EXPERTS, B, C, D), final_norm, C, final=(l == depth - 1), tm=tm_combine)
    return x
```
